```python
import math
import jax, jax.numpy as jnp
from jax import lax
import numpy as np


D_MODEL = 1024
BATCH = 8
SEQ = 2048
DEPTH = 2
DEC_BATCH = 8
DEC_SEQ = 32
PAST_LEN = 4096

CHUNK = 64
QBLK = 128
D_MIX = D_MODEL
A_WIDTH = D_MIX // 2
A_HEAD_DIM = 64
A_HEADS = A_WIDTH // (2 * A_HEAD_DIM)
A_VDIM = 2 * A_HEAD_DIM
ATTN_SCALE = A_HEAD_DIM ** -0.5
B_WIDTH = D_MIX // 4
POOL_WINDOWS = (2, 4, 8, 16)
N_POOL_GROUPS = len(POOL_WINDOWS)
POOL_GROUP = B_WIDTH // N_POOL_GROUPS
POOL_HIST = max(POOL_WINDOWS) - 1
C_WIDTH = D_MIX // 4
CONV_W = 3
D_FF = 2816
ROPE_THETA = 10000.0
EPS = 1e-6
NEG_INF = -1e30
D_IN = 3 * A_WIDTH + B_WIDTH + 3 * C_WIDTH
SPLITS = [A_WIDTH, 2 * A_WIDTH, 3 * A_WIDTH, 3 * A_WIDTH + B_WIDTH,
          3 * A_WIDTH + B_WIDTH + C_WIDTH, 3 * A_WIDTH + B_WIDTH + 2 * C_WIDTH]

kernel_name = 'hybrid_stream_diffattn_pool_conv_step'

F32 = jnp.float32


def rmsnorm(x, g):
    xf = x.astype(F32)
    y = xf * lax.rsqrt(jnp.mean(xf * xf, axis=-1, keepdims=True) + EPS)
    return (y * g.astype(F32)).astype(x.dtype)


def swiglu(x, wg, wu, wd):
    return (jax.nn.silu(x @ wg) * (x @ wu)) @ wd


def rope(x, pos):
    half = A_HEAD_DIM // 2
    inv = jnp.power(ROPE_THETA, -jnp.arange(half, dtype=F32) / half)
    ang = pos.astype(F32)[:, None] * inv[None, :]
    cos = jnp.cos(ang)[None, :, None, None, :]
    sin = jnp.sin(ang)[None, :, None, None, :]
    xf = x.astype(F32)
    x1, x2 = xf[..., :half], xf[..., half:]
    return jnp.concatenate([x1 * cos - x2 * sin, x2 * cos + x1 * sin], axis=-1).astype(x.dtype)


def diff_attend(q, k, v, mask, lam):
    s = jnp.einsum('bqhcd,bkhcd->bhcqk', q.astype(F32), k) * ATTN_SCALE
    if mask is not None:
        s = jnp.where(mask, s, NEG_INF)
    p = jax.nn.softmax(s, axis=-1)
    a = p[:, :, 0] - lam * p[:, :, 1]
    return jnp.einsum('bhqk,bkhe->bqhe', a, v)


def pool_mix(u_ext, pos, w_pool, scale):
    bsz, tot, _ = u_ext.shape
    t = tot - POOL_HIST
    uf = u_ext.astype(F32)
    cs = jnp.concatenate([jnp.zeros((bsz, 1, B_WIDTH), F32), jnp.cumsum(uf, axis=1)], axis=1)
    cur = uf[:, POOL_HIST:]
    outs = []
    for gi, w in enumerate(POOL_WINDOWS):
        lo, hi = gi * POOL_GROUP, (gi + 1) * POOL_GROUP
        wsum = cs[:, POOL_HIST + 1:, lo:hi] - cs[:, POOL_HIST + 1 - w:POOL_HIST + 1 - w + t, lo:hi]
        cnt = jnp.minimum(pos + 1, w).astype(F32)[None, :, None]
        outs.append(wsum / cnt - cur[..., lo:hi])
    d = jnp.stack(outs, axis=2)
    y = jnp.einsum('btgc,gce->btge', d, w_pool.astype(F32)).reshape(bsz, t, B_WIDTH)
    return (y * scale.astype(F32)).astype(u_ext.dtype)


def short_conv(v_ext, w):
    t = v_ext.shape[1] - (CONV_W - 1)
    y = w[0] * v_ext[:, 0:t]
    for j in range(1, CONV_W):
        y = y + w[j] * v_ext[:, j:j + t]
    return y


def run_trunk(x, pos, cache_k, cache_v, state_pool, state_conv, p):
    bsz, t, _ = x.shape
    new_k, new_v, new_pool, new_conv = [], [], [], []
    for l in range(DEPTH):
        h = rmsnorm(x, p['norm_ffn1'][l])
        x = x + 0.5 * swiglu(h, p['ffn1_gate'][l], p['ffn1_up'][l], p['ffn1_down'][l])
        h = rmsnorm(x, p['norm_mix'][l])
        z = h @ p['w_in'][l]
        q, k, v, u, hc, bg, cg = jnp.split(z, SPLITS, axis=-1)
        q = rope(q.reshape(bsz, t, A_HEADS, 2, A_HEAD_DIM), pos)
        k = rope(k.reshape(bsz, t, A_HEADS, 2, A_HEAD_DIM), pos)
        v = v.reshape(bsz, t, A_HEADS, A_VDIM)
        lam_init = 0.8 - 0.6 * math.exp(-0.3 * l)
        lq1, lk1 = p['lambda_q1'][l].astype(F32), p['lambda_k1'][l].astype(F32)
        lq2, lk2 = p['lambda_q2'][l].astype(F32), p['lambda_k2'][l].astype(F32)
        lam = jnp.exp(jnp.sum(lq1 * lk1)) - jnp.exp(jnp.sum(lq2 * lk2)) + lam_init
        cv = cg * hc
        if cache_k is None:
            kf, vf = k.astype(F32), v.astype(F32)
            nblk = t // QBLK
            qb = q.reshape(bsz, nblk, QBLK, A_HEADS, 2, A_HEAD_DIM).swapaxes(0, 1)
            key_chunk = jnp.arange(t) // CHUNK

            def blk(args):
                qi, bi = args
                q_chunk = (bi * QBLK + jnp.arange(QBLK)) // CHUNK
                mask = key_chunk[None, :] <= q_chunk[:, None]
                return diff_attend(qi, kf, vf, mask, lam)

            o = lax.map(blk, (qb, jnp.arange(nblk)))
            o = o.swapaxes(0, 1).reshape(bsz, t, A_HEADS, A_VDIM)
            pool_ext = jnp.concatenate([jnp.zeros((bsz, POOL_HIST, B_WIDTH), u.dtype), u], axis=1)
            conv_ext = jnp.concatenate([jnp.zeros((bsz, CONV_W - 1, C_WIDTH), cv.dtype), cv], axis=1)
        else:
            past = cache_k.shape[2]
            ck = cache_k[l].reshape(bsz, past, A_HEADS, 2, A_HEAD_DIM)
            k_all = jnp.concatenate([ck.astype(F32), k.astype(F32)], axis=1)
            v_all = jnp.concatenate([cache_v[l].astype(F32), v.astype(F32)], axis=1)
            o = diff_attend(q, k_all, v_all, None, lam)
            pool_ext = jnp.concatenate([state_pool[l].astype(u.dtype), u], axis=1)
            conv_ext = jnp.concatenate([state_conv[l].astype(cv.dtype), cv], axis=1)
        o = rmsnorm(o, p['subln'][l]) * (1.0 - lam_init)
        y_attn = o.reshape(bsz, t, A_WIDTH).astype(x.dtype)
        y_pool = pool_mix(pool_ext, pos, p['pool_w'][l], p['pool_scale'][l])
        y_conv = bg * short_conv(conv_ext, p['conv_w'][l])
        mix = jnp.concatenate([y_attn, y_pool, y_conv], axis=-1) @ p['w_out'][l]
        x = x + mix
        h = rmsnorm(x, p['norm_ffn2'][l])
        x = x + 0.5 * swiglu(h, p['ffn2_gate'][l], p['ffn2_up'][l], p['ffn2_down'][l])
        new_k.append(k.reshape(bsz, t, A_HEADS, 2 * A_HEAD_DIM))
        new_v.append(v)
        new_pool.append(pool_ext[:, -POOL_HIST:])
        new_conv.append(conv_ext[:, -(CONV_W - 1):])
    x = rmsnorm(x, p['final_norm'])
    return x, jnp.stack(new_k), jnp.stack(new_v), jnp.stack(new_pool), jnp.stack(new_conv)


def setup_inputs(seed: int = 0) -> dict:
    key = jax.random.key(seed)
    ks = jax.random.split(key, 32)

    def nrm(k, shape, s):
        return jax.random.normal(k, shape, F32) * s

    def gain(k, shape):
        return 1.0 + 0.1 * jax.random.normal(k, shape, F32)

    return {
        'x_prompt': nrm(ks[0], (BATCH, SEQ, D_MODEL), 1.0),
        'x_sample': nrm(ks[1], (DEC_BATCH, DEC_SEQ, D_MODEL), 1.0),
        'cache_k': nrm(ks[2], (DEPTH, DEC_BATCH, PAST_LEN, A_HEADS, 2 * A_HEAD_DIM), 1.0),
        'cache_v': nrm(ks[3], (DEPTH, DEC_BATCH, PAST_LEN, A_HEADS, A_VDIM), 1.0),
        'state_pool': nrm(ks[4], (DEPTH, DEC_BATCH, POOL_HIST, B_WIDTH), 1.0),
        'state_conv': nrm(ks[5], (DEPTH, DEC_BATCH, CONV_W - 1, C_WIDTH), 1.0),
        'norm_ffn1': gain(ks[6], (DEPTH, D_MODEL)),
        'ffn1_gate': nrm(ks[7], (DEPTH, D_MODEL, D_FF), D_MODEL ** -0.5),
        'ffn1_up': nrm(ks[8], (DEPTH, D_MODEL, D_FF), D_MODEL ** -0.5),
        'ffn1_down': nrm(ks[9], (DEPTH, D_FF, D_MODEL), D_FF ** -0.5),
        'norm_mix': gain(ks[10], (DEPTH, D_MODEL)),
        'w_in': nrm(ks[11], (DEPTH, D_MODEL, D_IN), D_MODEL ** -0.5),
        'lambda_q1': nrm(ks[12], (DEPTH, A_HEAD_DIM), 0.1),
        'lambda_k1': nrm(ks[13], (DEPTH, A_HEAD_DIM), 0.1),
        'lambda_q2': nrm(ks[14], (DEPTH, A_HEAD_DIM), 0.1),
        'lambda_k2': nrm(ks[15], (DEPTH, A_HEAD_DIM), 0.1),
        'subln': gain(ks[16], (DEPTH, A_VDIM)),
        'pool_w': nrm(ks[17], (DEPTH, N_POOL_GROUPS, POOL_GROUP, POOL_GROUP), POOL_GROUP ** -0.5),
        'pool_scale': gain(ks[18], (DEPTH, B_WIDTH)),
        'conv_w': nrm(ks[19], (DEPTH, CONV_W, C_WIDTH), CONV_W ** -0.5),
        'w_out': nrm(ks[20], (DEPTH, D_MIX, D_MODEL), D_MIX ** -0.5),
        'norm_ffn2': gain(ks[21], (DEPTH, D_MODEL)),
        'ffn2_gate': nrm(ks[22], (DEPTH, D_MODEL, D_FF), D_MODEL ** -0.5),
        'ffn2_up': nrm(ks[23], (DEPTH, D_MODEL, D_FF), D_MODEL ** -0.5),
        'ffn2_down': nrm(ks[24], (DEPTH, D_FF, D_MODEL), D_FF ** -0.5),
        'final_norm': gain(ks[25], (D_MODEL,)),
    }


def reference(x_prompt, x_sample, cache_k, cache_v, state_pool, state_conv,
              norm_ffn1, ffn1_gate, ffn1_up, ffn1_down, norm_mix, w_in,
              lambda_q1, lambda_k1, lambda_q2, lambda_k2, subln, pool_w, pool_scale, conv_w,
              w_out, norm_ffn2, ffn2_gate, ffn2_up, ffn2_down, final_norm):
    p = {
        'norm_ffn1': norm_ffn1, 'ffn1_gate': ffn1_gate, 'ffn1_up': ffn1_up, 'ffn1_down': ffn1_down,
        'norm_mix': norm_mix, 'w_in': w_in,
        'lambda_q1': lambda_q1, 'lambda_k1': lambda_k1, 'lambda_q2': lambda_q2, 'lambda_k2': lambda_k2,
        'subln': subln, 'pool_w': pool_w, 'pool_scale': pool_scale, 'conv_w': conv_w, 'w_out': w_out,
        'norm_ffn2': norm_ffn2, 'ffn2_gate': ffn2_gate, 'ffn2_up': ffn2_up, 'ffn2_down': ffn2_down,
        'final_norm': final_norm,
    }
    pos_prompt = jnp.arange(x_prompt.shape[1], dtype=jnp.int32)
    pos_sample = PAST_LEN + jnp.arange(x_sample.shape[1], dtype=jnp.int32)
    y_prompt, k_prompt, v_prompt, pool_prompt, conv_prompt = run_trunk(
        x_prompt, pos_prompt, None, None, None, None, p)
    y_sample, k_sample, v_sample, pool_sample, conv_sample = run_trunk(
        x_sample, pos_sample, cache_k, cache_v, state_pool, state_conv, p)
    return (y_prompt, y_sample, k_prompt, v_prompt, pool_prompt, conv_prompt,
            k_sample, v_sample, pool_sample, conv_sample)
```

```python
import functools
import math

import numpy as np
import jax
import jax.numpy as jnp
from jax import lax
from jax.experimental import pallas as pl
from jax.experimental.pallas import tpu as pltpu

F32 = jnp.float32
BF16 = jnp.bfloat16

D_MODEL = 1024
DEPTH = 2
PAST_LEN = 4096
CHUNK = 64
A_WIDTH = 512
A_HEAD_DIM = 64
A_HEADS = 4
A_VDIM = 128
ATTN_SCALE = A_HEAD_DIM ** -0.5
B_WIDTH = 256
POOL_WINDOWS = (2, 4, 8, 16)
POOL_GROUP = 64
POOL_HIST = 15
C_WIDTH = 256
CONV_W = 3
D_FF = 2816
ROPE_THETA = 10000.0
EPS = 1e-6
NEG_INF = -1e30
D_IN = 3 * A_WIDTH + B_WIDTH + 3 * C_WIDTH

HALO = 16
FF_CHUNK = 256
VMEM_LIMIT = 56 * 1024 * 1024


def _rms(x, g):
    return x * lax.rsqrt(jnp.mean(x * x, axis=-1, keepdims=True) + EPS) * g


def _const_spec(shape):
    return pl.BlockSpec(shape, lambda *_: (0,) * len(shape), pipeline_mode=pl.Buffered(1))


def _ffn_kernel(x_ref, g_ref, wg_ref, wu_ref, wd_ref, *rest, final_norm):
    if final_norm:
        fg_ref, o_ref = rest
    else:
        (o_ref,) = rest
    x = x_ref[...]
    h = _rms(x, g_ref[...]).astype(BF16)
    acc = jnp.zeros(x.shape, F32)
    for c in range(D_FF // FF_CHUNK):
        sl = slice(c * FF_CHUNK, (c + 1) * FF_CHUNK)
        gate = jnp.dot(h, wg_ref[:, sl], preferred_element_type=F32)
        up = jnp.dot(h, wu_ref[:, sl], preferred_element_type=F32)
        act = (gate * (1.0 / (1.0 + jnp.exp(-gate))) * up).astype(BF16)
        acc = acc + jnp.dot(act, wd_ref[sl, :], preferred_element_type=F32)
    y = x + 0.5 * acc
    if final_norm:
        y = _rms(y, fg_ref[...])
    o_ref[...] = y


def _ffn(x, g, wg, wu, wd, final_g=None, *, tm):
    n, d = x.shape
    assert n % tm == 0
    row = pl.BlockSpec((tm, d), lambda i: (i, 0))
    in_specs = [row, _const_spec((1, d)), _const_spec(wg.shape), _const_spec(wu.shape), _const_spec(wd.shape)]
    args = [x, g.reshape(1, d), wg, wu, wd]
    if final_g is not None:
        in_specs.append(_const_spec((1, d)))
        args.append(final_g.reshape(1, d))
    return pl.pallas_call(
        functools.partial(_ffn_kernel, final_norm=final_g is not None),
        grid=(n // tm,),
        in_specs=in_specs,
        out_specs=row,
        out_shape=jax.ShapeDtypeStruct((n, d), F32),
        compiler_params=pltpu.CompilerParams(dimension_semantics=("arbitrary",), vmem_limit_bytes=VMEM_LIMIT),
        name="ffn",
    )(*args)


def _inproj_kernel(x_ref, g_ref, w_ref, cos_ref, sa_ref, sb_ref, lq1_ref, lk1_ref, lq2_ref, lk2_ref,
                   q_ref, k_ref, kb_ref, v_ref, vb_ref, u_ref, cv_ref, bg_ref, lam_ref, *, lam_init):
    @pl.when(pl.program_id(0) == 0)
    def _():
        a = jnp.sum(lq1_ref[...] * lk1_ref[...], axis=-1, keepdims=True)
        b = jnp.sum(lq2_ref[...] * lk2_ref[...], axis=-1, keepdims=True)
        lam_ref[...] = jnp.broadcast_to(jnp.exp(a) - jnp.exp(b) + lam_init, lam_ref.shape)

    h = _rms(x_ref[...], g_ref[...]).astype(BF16)
    z = jnp.dot(h, w_ref[...], preferred_element_type=F32)
    cos, sa, sb = cos_ref[...], sa_ref[...], sb_ref[...]

    def rope(t):
        return t * cos + pltpu.roll(t, 96, 1) * sa + pltpu.roll(t, 32, 1) * sb

    for hd in range(A_HEADS):
        sl = slice(hd * A_VDIM, (hd + 1) * A_VDIM)
        qh = rope(z[:, hd * A_VDIM:(hd + 1) * A_VDIM])
        q_ref[:, sl] = (qh * ATTN_SCALE).astype(BF16)
        kh = rope(z[:, A_WIDTH + hd * A_VDIM:A_WIDTH + (hd + 1) * A_VDIM])
        k_ref[:, sl] = kh
        kb_ref[:, sl] = kh.astype(BF16)
    v = z[:, 2 * A_WIDTH:3 * A_WIDTH]
    v_ref[...] = v
    vb_ref[...] = v.astype(BF16)
    o = 3 * A_WIDTH
    u_ref[...] = z[:, o:o + B_WIDTH]
    hc = z[:, o + B_WIDTH:o + B_WIDTH + C_WIDTH]
    bg_ref[...] = z[:, o + B_WIDTH + C_WIDTH:o + B_WIDTH + 2 * C_WIDTH]
    cg = z[:, o + B_WIDTH + 2 * C_WIDTH:o + B_WIDTH + 3 * C_WIDTH]
    cv_ref[...] = cg * hc


def _rope_tables(pos):
    half = A_HEAD_DIM // 2
    inv = np.power(ROPE_THETA, -np.arange(half, dtype=np.float64) / half)
    ang = pos.astype(np.float64)[:, None] * inv[None, :]
    cos, sin, zero = np.cos(ang), np.sin(ang), np.zeros_like(ang)
    cos_t = np.concatenate([cos, cos, cos, cos], axis=1)
    sa = np.concatenate([-sin, zero, -sin, zero], axis=1)
    sb = np.concatenate([zero, sin, zero, sin], axis=1)
    return [jnp.asarray(t, F32) for t in (cos_t, sa, sb)]


def _inproj(x, g, w_in, tables, lams, lam_init, *, tm, seq):
    n, d = x.shape
    tps = seq // tm
    row = lambda w: pl.BlockSpec((tm, w), lambda i: (i, 0))
    tab = pl.BlockSpec((tm, A_VDIM), lambda i: (i % tps, 0))
    lam_spec = _const_spec((1, A_HEAD_DIM))
    out_shape = [
        jax.ShapeDtypeStruct((n, A_WIDTH), BF16),
        jax.ShapeDtypeStruct((n, A_WIDTH), F32),
        jax.ShapeDtypeStruct((n, A_WIDTH), BF16),
        jax.ShapeDtypeStruct((n, A_WIDTH), F32),
        jax.ShapeDtypeStruct((n, A_WIDTH), BF16),
        jax.ShapeDtypeStruct((n, B_WIDTH), F32),
        jax.ShapeDtypeStruct((n, C_WIDTH), F32),
        jax.ShapeDtypeStruct((n, C_WIDTH), F32),
        jax.ShapeDtypeStruct((8, A_VDIM), F32),
    ]
    out_specs = [row(A_WIDTH)] * 5 + [row(B_WIDTH), row(C_WIDTH), row(C_WIDTH),
                                      pl.BlockSpec((8, A_VDIM), lambda i: (0, 0))]
    return pl.pallas_call(
        functools.partial(_inproj_kernel, lam_init=lam_init),
        grid=(n // tm,),
        in_specs=[row(d), _const_spec((1, d)), _const_spec(w_in.shape), tab, tab, tab] + [lam_spec] * 4,
        out_specs=out_specs,
        out_shape=out_shape,
        compiler_params=pltpu.CompilerParams(dimension_semantics=("arbitrary",), vmem_limit_bytes=VMEM_LIMIT),
        name="inproj",
    )(x, g.reshape(1, d), w_in, *tables, *[v.reshape(1, A_HEAD_DIM) for v in lams])


def _stack_q(q):
    lane = lax.broadcasted_iota(jnp.int32, q.shape, 1)
    zero = jnp.zeros_like(q)
    return jnp.concatenate([jnp.where(lane < A_HEAD_DIM, q, zero), jnp.where(lane >= A_HEAD_DIM, q, zero)], axis=0)


def _qk(qs, k):
    return lax.dot_general(qs, k, (((1,), (1,)), ((), ())), preferred_element_type=F32)


def _attn_finish(acc, l, lam_ref, sg_ref, o_ref, tq, lam_init):
    o = acc[:tq] / l[:tq] - lam_ref[0:1, :] * (acc[tq:] / l[tq:])
    o = _rms(o, sg_ref[...]) * (1.0 - lam_init)
    o_ref[...] = o.astype(o_ref.dtype)


def _attn_prompt_kernel(q_ref, k_ref, v_ref, lam_ref, sg_ref, o_ref, *, tq, lam_init):
    qi = pl.program_id(2)
    qs = _stack_q(q_ref[...])

    def step(j, carry, diagonal):
        m, l, acc = carry
        start = pl.multiple_of(j * tq, tq)
        s = _qk(qs, k_ref[pl.ds(start, tq), :])
        if diagonal:
            qc = (lax.broadcasted_iota(jnp.int32, s.shape, 0) % tq) // CHUNK
            kc = lax.broadcasted_iota(jnp.int32, s.shape, 1) // CHUNK
            s = jnp.where(kc <= qc, s, NEG_INF)
        m_new = jnp.maximum(m, jnp.max(s, axis=-1, keepdims=True))
        alpha = jnp.exp(m - m_new)
        p = jnp.exp(s - m_new)
        l = alpha * l + jnp.sum(p, axis=-1, keepdims=True)
        acc = alpha * acc + jnp.dot(p.astype(BF16), v_ref[pl.ds(start, tq), :], preferred_element_type=F32)
        return m_new, l, acc

    init = (jnp.full((2 * tq, 1), NEG_INF, F32), jnp.zeros((2 * tq, 1), F32), jnp.zeros((2 * tq, A_VDIM), F32))
    carry = lax.fori_loop(0, qi, lambda j, c: step(j, c, False), init)
    _, l, acc = step(qi, carry, True)
    _attn_finish(acc, l, lam_ref, sg_ref, o_ref, tq, lam_init)


def _attn_prompt(q, k, v, lam, sg, lam_init, *, bsz, seq, tq):
    q, k, v = (t.reshape(bsz, seq, A_WIDTH) for t in (q, k, v))
    qspec = pl.BlockSpec((None, tq, A_VDIM), lambda b, h, i: (b, i, h))
    kspec = pl.BlockSpec((None, seq, A_VDIM), lambda b, h, i: (b, 0, h))
    out = pl.pallas_call(
        functools.partial(_attn_prompt_kernel, tq=tq, lam_init=lam_init),
        grid=(bsz, A_HEADS, seq // tq),
        in_specs=[qspec, kspec, kspec,
                  pl.BlockSpec((8, A_VDIM), lambda b, h, i: (0, 0)),
                  pl.BlockSpec((1, A_VDIM), lambda b, h, i: (0, 0))],
        out_specs=qspec,
        out_shape=jax.ShapeDtypeStruct((bsz, seq, A_WIDTH), BF16),
        compiler_params=pltpu.CompilerParams(dimension_semantics=("arbitrary",) * 3, vmem_limit_bytes=VMEM_LIMIT),
        name="attn_prompt",
    )(q, k, v, lam, sg.reshape(1, A_VDIM))
    return out.reshape(bsz * seq, A_WIDTH)


def _attn_sample_kernel(q_ref, kc_ref, vc_ref, kn_ref, vn_ref, lam_ref, sg_ref, o_ref, *, tq, lam_init):
    qs = _stack_q(q_ref[...])
    sc = _qk(qs, kc_ref[...].astype(BF16))
    sn = _qk(qs, kn_ref[...])
    m = jnp.maximum(jnp.max(sc, axis=-1, keepdims=True), jnp.max(sn, axis=-1, keepdims=True))
    pc = jnp.exp(sc - m)
    pn = jnp.exp(sn - m)
    l = jnp.sum(pc, axis=-1, keepdims=True) + jnp.sum(pn, axis=-1, keepdims=True)
    acc = (jnp.dot(pc.astype(BF16), vc_ref[...].astype(BF16), preferred_element_type=F32)
           + jnp.dot(pn.astype(BF16), vn_ref[...], preferred_element_type=F32))
    _attn_finish(acc, l, lam_ref, sg_ref, o_ref, tq, lam_init)


def _attn_sample(q, k, v, cache_k, cache_v, layer, lam, sg, lam_init, *, bsz, seq):
    past = cache_k.shape[2]
    q, k, v = (t.reshape(bsz, seq, A_WIDTH) for t in (q, k, v))
    ck = cache_k.reshape(DEPTH, bsz, past, A_WIDTH)
    cv = cache_v.reshape(DEPTH, bsz, past, A_WIDTH)
    nspec = pl.BlockSpec((None, seq, A_VDIM), lambda b, h: (b, 0, h))
    cspec = pl.BlockSpec((None, None, past, A_VDIM), lambda b, h: (layer, b, 0, h))
    out = pl.pallas_call(
        functools.partial(_attn_sample_kernel, tq=seq, lam_init=lam_init),
        grid=(bsz, A_HEADS),
        in_specs=[nspec, cspec, cspec, nspec, nspec,
                  pl.BlockSpec((8, A_VDIM), lambda b, h: (0, 0)),
                  pl.BlockSpec((1, A_VDIM), lambda b, h: (0, 0))],
        out_specs=nspec,
        out_shape=jax.ShapeDtypeStruct((bsz, seq, A_WIDTH), BF16),
        compiler_params=pltpu.CompilerParams(dimension_semantics=("arbitrary",) * 2, vmem_limit_bytes=VMEM_LIMIT),
        name="attn_sample",
    )(q, ck, cv, k, v, lam, sg.reshape(1, A_VDIM))
    return out.reshape(bsz * seq, A_WIDTH)


def _outproj_kernel(x_ref, ya_ref, u_ref, uh_ref, cv_ref, cvh_ref, bg_ref, wp_ref, ps_ref, cw_ref, wo_ref,
                    o_ref, uext_ref, cext_ref, *, tm, tps, pos_off, zero_first):
    it = pl.program_id(0) % tps
    uh, cvh = uh_ref[...], cvh_ref[...]
    if zero_first:
        keep = (it > 0).astype(F32)
        uh, cvh = uh * keep, cvh * keep
    cur = u_ref[...]
    uext_ref[0:HALO, :] = uh
    uext_ref[HALO:, :] = cur
    cv = cv_ref[...]
    cext_ref[0:HALO, :] = cvh
    cext_ref[HALO:, :] = cv

    run = cur
    sums = {}
    for j in range(1, max(POOL_WINDOWS)):
        run = run + uext_ref[pl.ds(HALO - j, tm), :]
        if j + 1 in POOL_WINDOWS:
            sums[j + 1] = run
    pos = pos_off + it * tm + lax.broadcasted_iota(jnp.int32, (tm, 1), 0)
    lane = lax.broadcasted_iota(jnp.int32, (tm, B_WIDTH), 1)
    d = None
    for gi, w in reversed(list(enumerate(POOL_WINDOWS))):
        mean = sums[w] / jnp.minimum(pos + 1, w).astype(F32)
        d = mean if d is None else jnp.where(lane < (gi + 1) * POOL_GROUP, mean, d)
    d = d - cur
    y_pool = jnp.dot(d.astype(BF16), wp_ref[...], preferred_element_type=F32) * ps_ref[...]

    cw = cw_ref[...]
    conv = cw[0:1, :] * cext_ref[pl.ds(HALO - 2, tm), :] + cw[1:2, :] * cext_ref[pl.ds(HALO - 1, tm), :]
    conv = conv + cw[2:3, :] * cv
    y_conv = bg_ref[...] * conv

    mixed = jnp.concatenate([ya_ref[...], y_pool.astype(BF16), y_conv.astype(BF16)], axis=-1)
    o_ref[...] = x_ref[...] + jnp.dot(mixed, wo_ref[...], preferred_element_type=F32)


def _outproj(x, ya, u, cv, bg, u_halo, cv_halo, halo_map, wp, ps, cw, wo, *, tm, seq, pos_off, zero_first):
    n, d = x.shape
    tps = seq // tm
    row = lambda w: pl.BlockSpec((tm, w), lambda i: (i, 0))
    halo = pl.BlockSpec((HALO, B_WIDTH), halo_map)
    return pl.pallas_call(
        functools.partial(_outproj_kernel, tm=tm, tps=tps, pos_off=pos_off, zero_first=zero_first),
        grid=(n // tm,),
        in_specs=[row(d), row(A_WIDTH), row(B_WIDTH), halo, row(C_WIDTH), halo, row(C_WIDTH),
                  _const_spec(wp.shape), _const_spec((1, B_WIDTH)), _const_spec(cw.shape), _const_spec(wo.shape)],
        out_specs=row(d),
        out_shape=jax.ShapeDtypeStruct((n, d), F32),
        scratch_shapes=[pltpu.VMEM((HALO + tm, B_WIDTH), F32), pltpu.VMEM((HALO + tm, C_WIDTH), F32)],
        compiler_params=pltpu.CompilerParams(dimension_semantics=("arbitrary",), vmem_limit_bytes=VMEM_LIMIT),
        name="outproj",
    )(x, ya, u, u_halo, cv, cv_halo, bg, wp, ps.reshape(1, B_WIDTH), cw, wo)


def _block_diag(w):
    g, c, _ = w.shape
    eye = jnp.eye(g, dtype=w.dtype)
    return (eye[:, None, :, None] * w[:, :, None, :]).reshape(g * c, g * c)


def _front_pad(state, rows):
    b, r, c = state.shape
    return jnp.pad(state, ((0, 0), (rows - r, 0), (0, 0))).reshape(b * rows, c)


def _trunk(x, pos0, cache, p, *, tm_ffn, tm_mix, tq):
    bsz, seq, d = x.shape
    n = bsz * seq
    x = x.reshape(n, d)
    tables = _rope_tables(pos0 + np.arange(seq))
    new_k, new_v, new_pool, new_conv = [], [], [], []
    for l in range(DEPTH):
        lam_init = 0.8 - 0.6 * math.exp(-0.3 * l)
        x = _ffn(x, p['norm_ffn1'][l], p['ffn1_gate'][l], p['ffn1_up'][l], p['ffn1_down'][l], tm=tm_ffn)
        lams = [p[name][l] for name in ('lambda_q1', 'lambda_k1', 'lambda_q2', 'lambda_k2')]
        q, k, kb, v, vb, u, cv, bg, lam = _inproj(x, p['norm_mix'][l], p['w_in'][l], tables, lams, lam_init,
                                                  tm=tm_mix, seq=seq)
        if cache is None:
            ya = _attn_prompt(q, kb, vb, lam, p['subln'][l], lam_init, bsz=bsz, seq=seq, tq=tq)
            u_halo, cv_halo = u, cv
            per = tm_mix // HALO
            halo_map = lambda i, per=per: (jnp.maximum(i * per - 1, 0), 0)
            pool_ext, conv_ext = u.reshape(bsz, seq, B_WIDTH), cv.reshape(bsz, seq, C_WIDTH)
        else:
            cache_k, cache_v, state_pool, state_conv = cache
            ya = _attn_sample(q, kb, vb, cache_k, cache_v, l, lam, p['subln'][l], lam_init, bsz=bsz, seq=seq)
            u_halo = _front_pad(state_pool[l], HALO)
            cv_halo = _front_pad(state_conv[l], HALO)
            halo_map = lambda i: (i, 0)
            pool_ext = jnp.concatenate([state_pool[l], u.reshape(bsz, seq, B_WIDTH)], axis=1)
            conv_ext = jnp.concatenate([state_conv[l], cv.reshape(bsz, seq, C_WIDTH)], axis=1)
        x = _outproj(x, ya, u, cv, bg, u_halo, cv_halo, halo_map,
                     p['pool_w_bd'][l], p['pool_scale'][l], p['conv_w'][l], p['w_out'][l],
                     tm=tm_mix, seq=seq, pos_off=pos0, zero_first=cache is None)
        x = _ffn(x, p['norm_ffn2'][l], p['ffn2_gate'][l], p['ffn2_up'][l], p['ffn2_down'][l],
                 p['final_norm'] if l == DEPTH - 1 else None, tm=tm_ffn)
        new_k.append(k.reshape(bsz, seq, A_HEADS, A_VDIM))
        new_v.append(v.reshape(bsz, seq, A_HEADS, A_VDIM))
        new_pool.append(pool_ext[:, -POOL_HIST:])
        new_conv.append(conv_ext[:, -(CONV_W - 1):])
    return (x.reshape(bsz, seq, d), jnp.stack(new_k), jnp.stack(new_v), jnp.stack(new_pool), jnp.stack(new_conv))


def kernel(x_prompt, x_sample, cache_k, cache_v, state_pool, state_conv, norm_ffn1, ffn1_gate, ffn1_up, ffn1_down,
           norm_mix, w_in, lambda_q1, lambda_k1, lambda_q2, lambda_k2, subln, pool_w, pool_scale, conv_w, w_out,
           norm_ffn2, ffn2_gate, ffn2_up, ffn2_down, final_norm):
    p = {
        'norm_ffn1': norm_ffn1, 'norm_mix': norm_mix, 'norm_ffn2': norm_ffn2, 'final_norm': final_norm,
        'lambda_q1': lambda_q1, 'lambda_k1': lambda_k1, 'lambda_q2': lambda_q2, 'lambda_k2': lambda_k2,
        'subln': subln, 'pool_scale': pool_scale, 'conv_w': conv_w,
        'ffn1_gate': ffn1_gate.astype(BF16), 'ffn1_up': ffn1_up.astype(BF16), 'ffn1_down': ffn1_down.astype(BF16),
        'ffn2_gate': ffn2_gate.astype(BF16), 'ffn2_up': ffn2_up.astype(BF16), 'ffn2_down': ffn2_down.astype(BF16),
        'w_in': w_in.astype(BF16), 'w_out': w_out.astype(BF16),
        'pool_w_bd': jnp.stack([_block_diag(pool_w[l]) for l in range(DEPTH)]).astype(BF16),
    }
    y_p, k_p, v_p, pool_p, conv_p = _trunk(x_prompt, 0, None, p, tm_ffn=512, tm_mix=512, tq=256)
    y_s, k_s, v_s, pool_s, conv_s = _trunk(x_sample, PAST_LEN, (cache_k, cache_v, state_pool, state_conv), p,
                                           tm_ffn=256, tm_mix=32, tq=None)
    return (y_p, y_s, k_p, v_p, pool_p, conv_p, k_s, v_s, pool_s, conv_s)
```

```python
import functools
import math

import numpy as np
import jax
import jax.numpy as jnp
from jax import lax
from jax.experimental import pallas as pl
from jax.experimental.pallas import tpu as pltpu

F32 = jnp.float32
BF16 = jnp.bfloat16

D_MODEL = 1024
DEPTH = 2
PAST_LEN = 4096
CHUNK = 64
A_WIDTH = 512
A_HEAD_DIM = 64
A_HEADS = 4
A_VDIM = 128
ATTN_SCALE = A_HEAD_DIM ** -0.5
B_WIDTH = 256
POOL_WINDOWS = (2, 4, 8, 16)
POOL_GROUP = 64
POOL_HIST = 15
C_WIDTH = 256
CONV_W = 3
D_FF = 2816
ROPE_THETA = 10000.0
EPS = 1e-6
NEG_INF = -1e30
D_IN = 3 * A_WIDTH + B_WIDTH + 3 * C_WIDTH

HALO = 16
FF_CHUNK = 256
VMEM_LIMIT = 56 * 1024 * 1024


def _rms(x, g):
    return x * lax.rsqrt(jnp.mean(x * x, axis=-1, keepdims=True) + EPS) * g


def _const_spec(shape):
    return pl.BlockSpec(shape, lambda *_: (0,) * len(shape), pipeline_mode=pl.Buffered(1))


def _ffn_kernel(x_ref, g_ref, wg_ref, wu_ref, wd_ref, *rest, final_norm):
    if final_norm:
        fg_ref, o_ref = rest
    else:
        (o_ref,) = rest
    x = x_ref[...]
    h = _rms(x, g_ref[...]).astype(BF16)
    acc = jnp.zeros(x.shape, F32)
    for c in range(D_FF // FF_CHUNK):
        sl = slice(c * FF_CHUNK, (c + 1) * FF_CHUNK)
        gate = jnp.dot(h, wg_ref[:, sl], preferred_element_type=F32)
        up = jnp.dot(h, wu_ref[:, sl], preferred_element_type=F32)
        act = (gate * (1.0 / (1.0 + jnp.exp(-gate))) * up).astype(BF16)
        acc = acc + jnp.dot(act, wd_ref[sl, :], preferred_element_type=F32)
    y = x + 0.5 * acc
    if final_norm:
        y = _rms(y, fg_ref[...])
    o_ref[...] = y


def _ffn(x, g, wg, wu, wd, final_g=None, *, tm):
    n, d = x.shape
    assert n % tm == 0
    row = pl.BlockSpec((tm, d), lambda i: (i, 0))
    in_specs = [row, _const_spec((1, d)), _const_spec(wg.shape), _const_spec(wu.shape), _const_spec(wd.shape)]
    args = [x, g.reshape(1, d), wg, wu, wd]
    if final_g is not None:
        in_specs.append(_const_spec((1, d)))
        args.append(final_g.reshape(1, d))
    return pl.pallas_call(
        functools.partial(_ffn_kernel, final_norm=final_g is not None),
        grid=(n // tm,),
        in_specs=in_specs,
        out_specs=row,
        out_shape=jax.ShapeDtypeStruct((n, d), F32),
        compiler_params=pltpu.CompilerParams(dimension_semantics=("arbitrary",), vmem_limit_bytes=VMEM_LIMIT),
        name="ffn",
    )(*args)


def _inproj_kernel(x_ref, g_ref, w_ref, cos_ref, sa_ref, sb_ref, lq1_ref, lk1_ref, lq2_ref, lk2_ref, *rest,
                   lam_init, has_prev, v_transposed):
    q_ref, k_ref, kb_ref, v_ref, vb_ref, u_ref, cv_ref, bg_ref, lam_ref = rest[2:] if has_prev else rest
    tm = x_ref.shape[0]

    @pl.when(pl.program_id(0) == 0)
    def _():
        a = jnp.sum(lq1_ref[...] * lk1_ref[...], axis=-1, keepdims=True)
        b = jnp.sum(lq2_ref[...] * lk2_ref[...], axis=-1, keepdims=True)
        lam_ref[...] = jnp.broadcast_to(jnp.exp(a) - jnp.exp(b) + lam_init, lam_ref.shape)

    h = _rms(x_ref[...], g_ref[...]).astype(BF16)
    z = jnp.dot(h, w_ref[...], preferred_element_type=F32)
    cos, sa, sb = cos_ref[...], sa_ref[...], sb_ref[...]

    def rope(t):
        return t * cos + pltpu.roll(t, 96, 1) * sa + pltpu.roll(t, 32, 1) * sb

    for hd in range(A_HEADS):
        sl = slice(hd * A_VDIM, (hd + 1) * A_VDIM)
        qh = rope(z[:, hd * A_VDIM:(hd + 1) * A_VDIM])
        q_ref[:, sl] = (qh * ATTN_SCALE).astype(BF16)
        kh = rope(z[:, A_WIDTH + hd * A_VDIM:A_WIDTH + (hd + 1) * A_VDIM])
        k_ref[pl.ds(hd, tm, stride=A_HEADS), :] = kh
        kb_ref[:, sl] = kh.astype(BF16)
    v = z[:, 2 * A_WIDTH:3 * A_WIDTH]
    for hd in range(A_HEADS):
        v_ref[pl.ds(hd, tm, stride=A_HEADS), :] = v[:, hd * A_VDIM:(hd + 1) * A_VDIM]
    vb_ref[...] = (v.T if v_transposed else v).astype(BF16)
    o = 3 * A_WIDTH
    u_ref[...] = z[:, o:o + B_WIDTH]
    hc = z[:, o + B_WIDTH:o + B_WIDTH + C_WIDTH]
    bg_ref[...] = z[:, o + B_WIDTH + C_WIDTH:o + B_WIDTH + 2 * C_WIDTH]
    cg = z[:, o + B_WIDTH + 2 * C_WIDTH:o + B_WIDTH + 3 * C_WIDTH]
    cv_ref[...] = cg * hc


def _rope_tables(pos):
    half = A_HEAD_DIM // 2
    inv = np.power(ROPE_THETA, -np.arange(half, dtype=np.float64) / half)
    ang = pos.astype(np.float64)[:, None] * inv[None, :]
    cos, sin, zero = np.cos(ang), np.sin(ang), np.zeros_like(ang)
    cos_t = np.concatenate([cos, cos, cos, cos], axis=1)
    sa = np.concatenate([-sin, zero, -sin, zero], axis=1)
    sb = np.concatenate([zero, sin, zero, sin], axis=1)
    return [jnp.asarray(t, F32) for t in (cos_t, sa, sb)]


def _inproj(x, g, w_in, tables, lams, lam_init, layer, kv_prev, *, tm, seq, v_transposed):
    n, d = x.shape
    tps = seq // tm
    if v_transposed:
        vb_shape = jax.ShapeDtypeStruct((n // seq, A_WIDTH, seq), BF16)
        vb_spec = pl.BlockSpec((None, A_WIDTH, tm), lambda i: (i // tps, 0, i % tps))
    else:
        vb_shape = jax.ShapeDtypeStruct((n, A_WIDTH), BF16)
        vb_spec = pl.BlockSpec((tm, A_WIDTH), lambda i: (i, 0))
    row = lambda w: pl.BlockSpec((tm, w), lambda i: (i, 0))
    tab = pl.BlockSpec((tm, A_VDIM), lambda i: (i % tps, 0))
    lam_spec = _const_spec((1, A_HEAD_DIM))
    leaf = jax.ShapeDtypeStruct((DEPTH, n * A_HEADS, A_VDIM), F32)
    leaf_spec = pl.BlockSpec((None, tm * A_HEADS, A_VDIM), lambda i: (layer, i, 0))
    out_shape = [
        jax.ShapeDtypeStruct((n, A_WIDTH), BF16),
        leaf,
        jax.ShapeDtypeStruct((n, A_WIDTH), BF16),
        leaf,
        vb_shape,
        jax.ShapeDtypeStruct((n, B_WIDTH), F32),
        jax.ShapeDtypeStruct((n, C_WIDTH), F32),
        jax.ShapeDtypeStruct((n, C_WIDTH), F32),
        jax.ShapeDtypeStruct((8, A_VDIM), F32),
    ]
    out_specs = [row(A_WIDTH), leaf_spec, row(A_WIDTH), leaf_spec, vb_spec, row(B_WIDTH), row(C_WIDTH),
                 row(C_WIDTH), pl.BlockSpec((8, A_VDIM), lambda i: (0, 0))]
    in_specs = [row(d), _const_spec((1, d)), _const_spec(w_in.shape), tab, tab, tab] + [lam_spec] * 4
    args = [x, g.reshape(1, d), w_in, *tables, *[v.reshape(1, A_HEAD_DIM) for v in lams]]
    aliases = {}
    if kv_prev is not None:
        aliases = {len(args): 1, len(args) + 1: 3}
        in_specs += [pl.BlockSpec(memory_space=pl.ANY)] * 2
        args += list(kv_prev)
    return pl.pallas_call(
        functools.partial(_inproj_kernel, lam_init=lam_init, has_prev=kv_prev is not None,
                          v_transposed=v_transposed),
        grid=(n // tm,),
        in_specs=in_specs,
        out_specs=out_specs,
        out_shape=out_shape,
        input_output_aliases=aliases,
        compiler_params=pltpu.CompilerParams(dimension_semantics=("arbitrary",), vmem_limit_bytes=VMEM_LIMIT),
        name="inproj",
    )(*args)


def _stack_q(q):
    lane = lax.broadcasted_iota(jnp.int32, q.shape, 1)
    zero = jnp.zeros_like(q)
    return jnp.concatenate([jnp.where(lane < A_HEAD_DIM, q, zero), jnp.where(lane >= A_HEAD_DIM, q, zero)], axis=0)


def _qk(qs, k):
    return lax.dot_general(qs, k, (((1,), (1,)), ((), ())), preferred_element_type=F32)


def _attn_finish(acc, l, lam_ref, sg_ref, tq, lam_init):
    o = acc[:tq] / l[:tq] - lam_ref[0:1, :] * (acc[tq:] / l[tq:])
    return _rms(o, sg_ref[...]) * (1.0 - lam_init)


def _attn_prompt_kernel(q_ref, k_ref, vt_ref, lam_ref, sg_ref, o_ref, qs_ref, m_ref, l_ref, acc_ref, *, tq, lam_init):
    qi = pl.program_id(1)
    for hd in range(A_HEADS):
        qs_ref[hd] = _stack_q(q_ref[:, hd * A_VDIM:(hd + 1) * A_VDIM])
    m_ref[...] = jnp.full(m_ref.shape, NEG_INF, F32)
    l_ref[...] = jnp.zeros(l_ref.shape, F32)
    acc_ref[...] = jnp.zeros(acc_ref.shape, F32)

    def step(j, diagonal):
        start = pl.multiple_of(j * tq, tq)
        heads = [slice(hd * A_VDIM, (hd + 1) * A_VDIM) for hd in range(A_HEADS)]
        scores = [_qk(k_ref[pl.ds(start, tq), hs], qs_ref[hd]) for hd, hs in enumerate(heads)]
        for hd, hs in enumerate(heads):
            st = scores[hd]
            if diagonal:
                kc = lax.broadcasted_iota(jnp.int32, st.shape, 0) // CHUNK
                qc = (lax.broadcasted_iota(jnp.int32, st.shape, 1) % tq) // CHUNK
                st = jnp.where(kc <= qc, st, NEG_INF)
            m_old = m_ref[hd]
            m_new = jnp.maximum(m_old, jnp.max(st, axis=0, keepdims=True))
            alpha = jnp.exp(m_old - m_new)
            pt = jnp.exp(st - m_new)
            m_ref[hd] = m_new
            l_ref[hd] = alpha * l_ref[hd] + jnp.sum(pt, axis=0, keepdims=True)
            acc_ref[hd] = alpha * acc_ref[hd] + jnp.dot(vt_ref[hs, pl.ds(start, tq)], pt.astype(BF16),
                                                        preferred_element_type=F32)

    def body(j, carry):
        step(j, False)
        return carry

    lax.fori_loop(0, qi, body, 0)
    step(qi, True)
    lam = lam_ref[0:1, 0:1]
    for hd in range(A_HEADS):
        acc, l = acc_ref[hd], l_ref[hd]
        ot = acc[:, :tq] / l[:, :tq] - lam * (acc[:, tq:] / l[:, tq:])
        ot = ot * lax.rsqrt(jnp.mean(ot * ot, axis=0, keepdims=True) + EPS) * sg_ref[...] * (1.0 - lam_init)
        o_ref[:, hd * A_VDIM:(hd + 1) * A_VDIM] = ot.T.astype(o_ref.dtype)


def _attn_prompt(q, k, vt, lam, sg, lam_init, *, bsz, seq, tq):
    q, k = (t.reshape(bsz, seq, A_WIDTH) for t in (q, k))
    qspec = pl.BlockSpec((None, tq, A_WIDTH), lambda b, i: (b, i, 0))
    kspec = pl.BlockSpec((None, seq, A_WIDTH), lambda b, i: (b, 0, 0))
    vspec = pl.BlockSpec((None, A_WIDTH, seq), lambda b, i: (b, 0, 0))
    out = pl.pallas_call(
        functools.partial(_attn_prompt_kernel, tq=tq, lam_init=lam_init),
        grid=(bsz, seq // tq),
        in_specs=[qspec, kspec, vspec,
                  pl.BlockSpec((8, A_VDIM), lambda b, i: (0, 0)),
                  pl.BlockSpec((A_VDIM, 1), lambda b, i: (0, 0))],
        out_specs=qspec,
        out_shape=jax.ShapeDtypeStruct((bsz, seq, A_WIDTH), BF16),
        scratch_shapes=[pltpu.VMEM((A_HEADS, 2 * tq, A_VDIM), BF16),
                        pltpu.VMEM((A_HEADS, 1, 2 * tq), F32),
                        pltpu.VMEM((A_HEADS, 1, 2 * tq), F32),
                        pltpu.VMEM((A_HEADS, A_VDIM, 2 * tq), F32)],
        compiler_params=pltpu.CompilerParams(dimension_semantics=("arbitrary",) * 2, vmem_limit_bytes=VMEM_LIMIT),
        name="attn_prompt",
    )(q, k, vt, lam, sg.reshape(A_VDIM, 1))
    return out.reshape(bsz * seq, A_WIDTH)


def _attn_sample_kernel(q_ref, kc_ref, vc_ref, kn_ref, vn_ref, lam_ref, sg_ref, o_ref, *, tq, past, lam_init):
    for hd in range(A_HEADS):
        hs = slice(hd * A_VDIM, (hd + 1) * A_VDIM)
        qs = _stack_q(q_ref[:, hs])
        sc = _qk(qs, kc_ref[pl.ds(hd, past, stride=A_HEADS), :].astype(BF16))
        sn = _qk(qs, kn_ref[:, hs])
        m = jnp.maximum(jnp.max(sc, axis=-1, keepdims=True), jnp.max(sn, axis=-1, keepdims=True))
        pc = jnp.exp(sc - m)
        pn = jnp.exp(sn - m)
        l = jnp.sum(pc, axis=-1, keepdims=True) + jnp.sum(pn, axis=-1, keepdims=True)
        vc = vc_ref[pl.ds(hd, past, stride=A_HEADS), :].astype(BF16)
        acc = (jnp.dot(pc.astype(BF16), vc, preferred_element_type=F32)
               + jnp.dot(pn.astype(BF16), vn_ref[:, hs], preferred_element_type=F32))
        o_ref[:, hs] = _attn_finish(acc, l, lam_ref, sg_ref, tq, lam_init).astype(o_ref.dtype)


def _attn_sample(q, k, v, cache_k, cache_v, layer, lam, sg, lam_init, *, bsz, seq):
    past = cache_k.shape[2]
    q, k, v = (t.reshape(bsz, seq, A_WIDTH) for t in (q, k, v))
    ck = cache_k.reshape(DEPTH, bsz, past * A_HEADS, A_VDIM)
    cv = cache_v.reshape(DEPTH, bsz, past * A_HEADS, A_VDIM)
    nspec = pl.BlockSpec((None, seq, A_WIDTH), lambda b: (b, 0, 0))
    cspec = pl.BlockSpec((None, None, past * A_HEADS, A_VDIM), lambda b: (layer, b, 0, 0))
    out = pl.pallas_call(
        functools.partial(_attn_sample_kernel, tq=seq, past=past, lam_init=lam_init),
        grid=(bsz,),
        in_specs=[nspec, cspec, cspec, nspec, nspec,
                  pl.BlockSpec((8, A_VDIM), lambda b: (0, 0)),
                  pl.BlockSpec((1, A_VDIM), lambda b: (0, 0))],
        out_specs=nspec,
        out_shape=jax.ShapeDtypeStruct((bsz, seq, A_WIDTH), BF16),
        compiler_params=pltpu.CompilerParams(dimension_semantics=("arbitrary",), vmem_limit_bytes=VMEM_LIMIT),
        name="attn_sample",
    )(q, ck, cv, k, v, lam, sg.reshape(1, A_VDIM))
    return out.reshape(bsz * seq, A_WIDTH)


def _outproj_kernel(x_ref, ya_ref, u_ref, uh_ref, cv_ref, cvh_ref, bg_ref, wp_ref, ps_ref, cw_ref, wo_ref,
                    o_ref, uext_ref, cext_ref, *, tm, tps, pos_off, zero_first):
    it = pl.program_id(0) % tps
    uh, cvh = uh_ref[...], cvh_ref[...]
    if zero_first:
        keep = (it > 0).astype(F32)
        uh, cvh = uh * keep, cvh * keep
    cur = u_ref[...]
    uext_ref[0:HALO, :] = uh
    uext_ref[HALO:, :] = cur
    cv = cv_ref[...]
    cext_ref[0:HALO, :] = cvh
    cext_ref[HALO:, :] = cv

    run = cur
    sums = {}
    for j in range(1, max(POOL_WINDOWS)):
        run = run + uext_ref[pl.ds(HALO - j, tm), :]
        if j + 1 in POOL_WINDOWS:
            sums[j + 1] = run
    pos = pos_off + it * tm + lax.broadcasted_iota(jnp.int32, (tm, 1), 0)
    lane = lax.broadcasted_iota(jnp.int32, (tm, B_WIDTH), 1)
    d = None
    for gi, w in reversed(list(enumerate(POOL_WINDOWS))):
        mean = sums[w] / jnp.minimum(pos + 1, w).astype(F32)
        d = mean if d is None else jnp.where(lane < (gi + 1) * POOL_GROUP, mean, d)
    d = d - cur
    y_pool = jnp.dot(d.astype(BF16), wp_ref[...], preferred_element_type=F32) * ps_ref[...]

    cw = cw_ref[...]
    conv = cw[0:1, :] * cext_ref[pl.ds(HALO - 2, tm), :] + cw[1:2, :] * cext_ref[pl.ds(HALO - 1, tm), :]
    conv = conv + cw[2:3, :] * cv
    y_conv = bg_ref[...] * conv

    mixed = jnp.concatenate([ya_ref[...], y_pool.astype(BF16), y_conv.astype(BF16)], axis=-1)
    o_ref[...] = x_ref[...] + jnp.dot(mixed, wo_ref[...], preferred_element_type=F32)


def _outproj(x, ya, u, cv, bg, u_halo, cv_halo, halo_map, wp, ps, cw, wo, *, tm, seq, pos_off, zero_first):
    n, d = x.shape
    tps = seq // tm
    row = lambda w: pl.BlockSpec((tm, w), lambda i: (i, 0))
    halo = pl.BlockSpec((HALO, B_WIDTH), halo_map)
    return pl.pallas_call(
        functools.partial(_outproj_kernel, tm=tm, tps=tps, pos_off=pos_off, zero_first=zero_first),
        grid=(n // tm,),
        in_specs=[row(d), row(A_WIDTH), row(B_WIDTH), halo, row(C_WIDTH), halo, row(C_WIDTH),
                  _const_spec(wp.shape), _const_spec((1, B_WIDTH)), _const_spec(cw.shape), _const_spec(wo.shape)],
        out_specs=row(d),
        out_shape=jax.ShapeDtypeStruct((n, d), F32),
        scratch_shapes=[pltpu.VMEM((HALO + tm, B_WIDTH), F32), pltpu.VMEM((HALO + tm, C_WIDTH), F32)],
        compiler_params=pltpu.CompilerParams(dimension_semantics=("arbitrary",), vmem_limit_bytes=VMEM_LIMIT),
        name="outproj",
    )(x, ya, u, u_halo, cv, cv_halo, bg, wp, ps.reshape(1, B_WIDTH), cw, wo)


def _block_diag(w):
    g, c, _ = w.shape
    eye = jnp.eye(g, dtype=w.dtype)
    return (eye[:, None, :, None] * w[:, :, None, :]).reshape(g * c, g * c)


def _front_pad(state, rows):
    b, r, c = state.shape
    return jnp.pad(state, ((0, 0), (rows - r, 0), (0, 0))).reshape(b * rows, c)


def _trunk(x, pos0, cache, p, *, tm_ffn, tm_mix, tq):
    bsz, seq, d = x.shape
    n = bsz * seq
    x = x.reshape(n, d)
    tables = _rope_tables(pos0 + np.arange(seq))
    kv_leaves, new_pool, new_conv = None, [], []
    for l in range(DEPTH):
        lam_init = 0.8 - 0.6 * math.exp(-0.3 * l)
        x = _ffn(x, p['norm_ffn1'][l], p['ffn1_gate'][l], p['ffn1_up'][l], p['ffn1_down'][l], tm=tm_ffn)
        lams = [p[name][l] for name in ('lambda_q1', 'lambda_k1', 'lambda_q2', 'lambda_k2')]
        q, k_all, kb, v_all, vb, u, cv, bg, lam = _inproj(x, p['norm_mix'][l], p['w_in'][l], tables, lams, lam_init,
                                                          l, kv_leaves, tm=tm_mix, seq=seq,
                                                          v_transposed=cache is None)
        kv_leaves = (k_all, v_all)
        if cache is None:
            ya = _attn_prompt(q, kb, vb, lam, p['subln'][l], lam_init, bsz=bsz, seq=seq, tq=tq)
            u_halo, cv_halo = u, cv
            per = tm_mix // HALO
            halo_map = lambda i, per=per: (jnp.maximum(i * per - 1, 0), 0)
            pool_ext, conv_ext = u.reshape(bsz, seq, B_WIDTH), cv.reshape(bsz, seq, C_WIDTH)
        else:
            cache_k, cache_v, state_pool, state_conv = cache
            ya = _attn_sample(q, kb, vb, cache_k, cache_v, l, lam, p['subln'][l], lam_init, bsz=bsz, seq=seq)
            u_halo = _front_pad(state_pool[l], HALO)
            cv_halo = _front_pad(state_conv[l], HALO)
            halo_map = lambda i: (i, 0)
            pool_ext = jnp.concatenate([state_pool[l], u.reshape(bsz, seq, B_WIDTH)], axis=1)
            conv_ext = jnp.concatenate([state_conv[l], cv.reshape(bsz, seq, C_WIDTH)], axis=1)
        x = _outproj(x, ya, u, cv, bg, u_halo, cv_halo, halo_map,
                     p['pool_w_bd'][l], p['pool_scale'][l], p['conv_w'][l], p['w_out'][l],
                     tm=tm_mix, seq=seq, pos_off=pos0, zero_first=cache is None)
        x = _ffn(x, p['norm_ffn2'][l], p['ffn2_gate'][l], p['ffn2_up'][l], p['ffn2_down'][l],
                 p['final_norm'] if l == DEPTH - 1 else None, tm=tm_ffn)
        new_pool.append(pool_ext[:, -POOL_HIST:])
        new_conv.append(conv_ext[:, -(CONV_W - 1):])
    new_k, new_v = (t.reshape(DEPTH, bsz, seq, A_HEADS, A_VDIM) for t in kv_leaves)
    return (x.reshape(bsz, seq, d), new_k, new_v, jnp.stack(new_pool), jnp.stack(new_conv))


def kernel(x_prompt, x_sample, cache_k, cache_v, state_pool, state_conv, norm_ffn1, ffn1_gate, ffn1_up, ffn1_down,
           norm_mix, w_in, lambda_q1, lambda_k1, lambda_q2, lambda_k2, subln, pool_w, pool_scale, conv_w, w_out,
           norm_ffn2, ffn2_gate, ffn2_up, ffn2_down, final_norm):
    p = {
        'norm_ffn1': norm_ffn1, 'norm_mix': norm_mix, 'norm_ffn2': norm_ffn2, 'final_norm': final_norm,
        'lambda_q1': lambda_q1, 'lambda_k1': lambda_k1, 'lambda_q2': lambda_q2, 'lambda_k2': lambda_k2,
        'subln': subln, 'pool_scale': pool_scale, 'conv_w': conv_w,
        'ffn1_gate': ffn1_gate.astype(BF16), 'ffn1_up': ffn1_up.astype(BF16), 'ffn1_down': ffn1_down.astype(BF16),
        'ffn2_gate': ffn2_gate.astype(BF16), 'ffn2_up': ffn2_up.astype(BF16), 'ffn2_down': ffn2_down.astype(BF16),
        'w_in': w_in.astype(BF16), 'w_out': w_out.astype(BF16),
        'pool_w_bd': jnp.stack([_block_diag(pool_w[l]) for l in range(DEPTH)]).astype(BF16),
    }
    y_p, k_p, v_p, pool_p, conv_p = _trunk(x_prompt, 0, None, p, tm_ffn=512, tm_mix=512, tq=256)
    y_s, k_s, v_s, pool_s, conv_s = _trunk(x_sample, PAST_LEN, (cache_k, cache_v, state_pool, state_conv), p,
                                           tm_ffn=256, tm_mix=32, tq=None)
    return (y_p, y_s, k_p, v_p, pool_p, conv_p, k_s, v_s, pool_s, conv_s)
```

```python
import functools
import math

import numpy as np
import jax
import jax.numpy as jnp
from jax import lax
from jax.experimental import pallas as pl
from jax.experimental.pallas import tpu as pltpu

F32 = jnp.float32
BF16 = jnp.bfloat16

D_MODEL = 1024
DEPTH = 2
PAST_LEN = 4096
CHUNK = 64
A_WIDTH = 512
A_HEAD_DIM = 64
A_HEADS = 4
A_VDIM = 128
ATTN_SCALE = A_HEAD_DIM ** -0.5
B_WIDTH = 256
POOL_WINDOWS = (2, 4, 8, 16)
POOL_GROUP = 64
POOL_HIST = 15
C_WIDTH = 256
CONV_W = 3
D_FF = 2816
ROPE_THETA = 10000.0
EPS = 1e-6
NEG_INF = -1e30
D_IN = 3 * A_WIDTH + B_WIDTH + 3 * C_WIDTH

LOG2E = math.log2(math.e)
BF16_ROWS = 16
VT_ROWS = A_VDIM + BF16_ROWS
HALO = 16
FF_CHUNK = 256
VMEM_LIMIT = 56 * 1024 * 1024


def _rms(x, g):
    return x * lax.rsqrt(jnp.mean(x * x, axis=-1, keepdims=True) + EPS) * g


def _const_spec(shape):
    return pl.BlockSpec(shape, lambda *_: (0,) * len(shape), pipeline_mode=pl.Buffered(1))


def _ffn_kernel(x_ref, g_ref, wg_ref, wu_ref, wd_ref, *rest, final_norm):
    if final_norm:
        fg_ref, o_ref = rest
    else:
        (o_ref,) = rest
    x = x_ref[...]
    h = _rms(x, g_ref[...]).astype(BF16)
    acc = jnp.zeros(x.shape, F32)
    for c in range(D_FF // FF_CHUNK):
        sl = slice(c * FF_CHUNK, (c + 1) * FF_CHUNK)
        gate = jnp.dot(h, wg_ref[:, sl], preferred_element_type=F32)
        up = jnp.dot(h, wu_ref[:, sl], preferred_element_type=F32)
        act = (gate * (1.0 / (1.0 + jnp.exp(-gate))) * up).astype(BF16)
        acc = acc + jnp.dot(act, wd_ref[sl, :], preferred_element_type=F32)
    y = x + 0.5 * acc
    if final_norm:
        y = _rms(y, fg_ref[...])
    o_ref[...] = y


def _ffn(x, g, wg, wu, wd, final_g=None, *, tm):
    n, d = x.shape
    assert n % tm == 0
    row = pl.BlockSpec((tm, d), lambda i: (i, 0))
    in_specs = [row, _const_spec((1, d)), _const_spec(wg.shape), _const_spec(wu.shape), _const_spec(wd.shape)]
    args = [x, g.reshape(1, d), wg, wu, wd]
    if final_g is not None:
        in_specs.append(_const_spec((1, d)))
        args.append(final_g.reshape(1, d))
    return pl.pallas_call(
        functools.partial(_ffn_kernel, final_norm=final_g is not None),
        grid=(n // tm,),
        in_specs=in_specs,
        out_specs=row,
        out_shape=jax.ShapeDtypeStruct((n, d), F32),
        compiler_params=pltpu.CompilerParams(dimension_semantics=("arbitrary",), vmem_limit_bytes=VMEM_LIMIT),
        name="ffn",
    )(*args)


def _inproj_kernel(x_ref, g_ref, w_ref, cos_ref, sa_ref, sb_ref, lq1_ref, lk1_ref, lq2_ref, lk2_ref, *rest,
                   lam_init, has_prev, v_transposed):
    q_ref, k_ref, kb_ref, v_ref, vb_ref, u_ref, cv_ref, bg_ref, lam_ref = rest[2:] if has_prev else rest
    tm = x_ref.shape[0]

    @pl.when(pl.program_id(0) == 0)
    def _():
        a = jnp.sum(lq1_ref[...] * lk1_ref[...], axis=-1, keepdims=True)
        b = jnp.sum(lq2_ref[...] * lk2_ref[...], axis=-1, keepdims=True)
        lam_ref[...] = jnp.broadcast_to(jnp.exp(a) - jnp.exp(b) + lam_init, lam_ref.shape)

    h = _rms(x_ref[...], g_ref[...]).astype(BF16)
    z = jnp.dot(h, w_ref[...], preferred_element_type=F32)
    cos, sa, sb = cos_ref[...], sa_ref[...], sb_ref[...]

    def rope(t):
        return t * cos + pltpu.roll(t, 96, 1) * sa + pltpu.roll(t, 32, 1) * sb

    for hd in range(A_HEADS):
        sl = slice(hd * A_VDIM, (hd + 1) * A_VDIM)
        qh = rope(z[:, hd * A_VDIM:(hd + 1) * A_VDIM])
        q_ref[:, sl] = (qh * (ATTN_SCALE * LOG2E)).astype(BF16)
        kh = rope(z[:, A_WIDTH + hd * A_VDIM:A_WIDTH + (hd + 1) * A_VDIM])
        k_ref[pl.ds(hd, tm, stride=A_HEADS), :] = kh
        kb_ref[:, sl] = kh.astype(BF16)
    v = z[:, 2 * A_WIDTH:3 * A_WIDTH]
    for hd in range(A_HEADS):
        v_ref[pl.ds(hd, tm, stride=A_HEADS), :] = v[:, hd * A_VDIM:(hd + 1) * A_VDIM]
    if v_transposed:
        vt = v.T.astype(BF16)
        for hd in range(A_HEADS):
            vb_ref[hd, 0:A_VDIM, :] = vt[hd * A_VDIM:(hd + 1) * A_VDIM, :]
            vb_ref[hd, A_VDIM:VT_ROWS, :] = jnp.ones((BF16_ROWS, tm), BF16)
    else:
        vb_ref[...] = v.astype(BF16)
    o = 3 * A_WIDTH
    u_ref[...] = z[:, o:o + B_WIDTH]
    hc = z[:, o + B_WIDTH:o + B_WIDTH + C_WIDTH]
    bg_ref[...] = z[:, o + B_WIDTH + C_WIDTH:o + B_WIDTH + 2 * C_WIDTH]
    cg = z[:, o + B_WIDTH + 2 * C_WIDTH:o + B_WIDTH + 3 * C_WIDTH]
    cv_ref[...] = cg * hc


def _rope_tables(pos):
    half = A_HEAD_DIM // 2
    inv = np.power(ROPE_THETA, -np.arange(half, dtype=np.float64) / half)
    ang = pos.astype(np.float64)[:, None] * inv[None, :]
    cos, sin, zero = np.cos(ang), np.sin(ang), np.zeros_like(ang)
    cos_t = np.concatenate([cos, cos, cos, cos], axis=1)
    sa = np.concatenate([-sin, zero, -sin, zero], axis=1)
    sb = np.concatenate([zero, sin, zero, sin], axis=1)
    return [jnp.asarray(t, F32) for t in (cos_t, sa, sb)]


def _inproj(x, g, w_in, tables, lams, lam_init, layer, kv_prev, *, tm, seq, v_transposed):
    n, d = x.shape
    tps = seq // tm
    if v_transposed:
        vb_shape = jax.ShapeDtypeStruct((n // seq, A_HEADS, VT_ROWS, seq), BF16)
        vb_spec = pl.BlockSpec((None, A_HEADS, VT_ROWS, tm), lambda i: (i // tps, 0, 0, i % tps))
    else:
        vb_shape = jax.ShapeDtypeStruct((n, A_WIDTH), BF16)
        vb_spec = pl.BlockSpec((tm, A_WIDTH), lambda i: (i, 0))
    row = lambda w: pl.BlockSpec((tm, w), lambda i: (i, 0))
    tab = pl.BlockSpec((tm, A_VDIM), lambda i: (i % tps, 0))
    lam_spec = _const_spec((1, A_HEAD_DIM))
    leaf = jax.ShapeDtypeStruct((DEPTH, n * A_HEADS, A_VDIM), F32)
    leaf_spec = pl.BlockSpec((None, tm * A_HEADS, A_VDIM), lambda i: (layer, i, 0))
    out_shape = [
        jax.ShapeDtypeStruct((n, A_WIDTH), BF16),
        leaf,
        jax.ShapeDtypeStruct((n, A_WIDTH), BF16),
        leaf,
        vb_shape,
        jax.ShapeDtypeStruct((n, B_WIDTH), F32),
        jax.ShapeDtypeStruct((n, C_WIDTH), F32),
        jax.ShapeDtypeStruct((n, C_WIDTH), F32),
        jax.ShapeDtypeStruct((8, A_VDIM), F32),
    ]
    out_specs = [row(A_WIDTH), leaf_spec, row(A_WIDTH), leaf_spec, vb_spec, row(B_WIDTH), row(C_WIDTH),
                 row(C_WIDTH), pl.BlockSpec((8, A_VDIM), lambda i: (0, 0))]
    in_specs = [row(d), _const_spec((1, d)), _const_spec(w_in.shape), tab, tab, tab] + [lam_spec] * 4
    args = [x, g.reshape(1, d), w_in, *tables, *[v.reshape(1, A_HEAD_DIM) for v in lams]]
    aliases = {}
    if kv_prev is not None:
        aliases = {len(args): 1, len(args) + 1: 3}
        in_specs += [pl.BlockSpec(memory_space=pl.ANY)] * 2
        args += list(kv_prev)
    return pl.pallas_call(
        functools.partial(_inproj_kernel, lam_init=lam_init, has_prev=kv_prev is not None,
                          v_transposed=v_transposed),
        grid=(n // tm,),
        in_specs=in_specs,
        out_specs=out_specs,
        out_shape=out_shape,
        input_output_aliases=aliases,
        compiler_params=pltpu.CompilerParams(dimension_semantics=("arbitrary",), vmem_limit_bytes=VMEM_LIMIT),
        name="inproj",
    )(*args)


def _stack_q(q):
    lane = lax.broadcasted_iota(jnp.int32, q.shape, 1)
    zero = jnp.zeros_like(q)
    return jnp.concatenate([jnp.where(lane < A_HEAD_DIM, q, zero), jnp.where(lane >= A_HEAD_DIM, q, zero)], axis=0)


def _qk(qs, k):
    return lax.dot_general(qs, k, (((1,), (1,)), ((), ())), preferred_element_type=F32)


def _attn_finish(acc, l, lam_ref, sg_ref, tq, lam_init):
    o = acc[:tq] / l[:tq] - lam_ref[0:1, :] * (acc[tq:] / l[tq:])
    return _rms(o, sg_ref[...]) * (1.0 - lam_init)


def _attn_prompt_kernel(q_ref, k_ref, vt_ref, lam_ref, sg_ref, o_ref, qs_ref, s_ref, m_ref, acc_ref, *, tq, lam_init):
    qi = pl.program_id(1)
    for hd in range(A_HEADS):
        qs_ref[hd] = _stack_q(q_ref[:, hd * A_VDIM:(hd + 1) * A_VDIM])
    m_ref[...] = jnp.full(m_ref.shape, NEG_INF, F32)
    acc_ref[...] = jnp.zeros(acc_ref.shape, F32)

    def scores_into(j, hd):
        start = pl.multiple_of(j * tq, tq)
        s_ref[hd] = _qk(k_ref[pl.ds(start, tq), hd * A_VDIM:(hd + 1) * A_VDIM], qs_ref[hd])

    def softmax_pv(j, hd, visible):
        start = pl.multiple_of(j * tq, tq)
        st = s_ref[hd]
        if visible is not None:
            st = jnp.where(visible, st, NEG_INF)
        m_old = m_ref[hd]
        m_new = jnp.maximum(m_old, jnp.max(st, axis=0, keepdims=True))
        alpha = jnp.exp2(m_old - m_new)
        pt = jnp.exp2(st - m_new)
        m_ref[hd] = m_new
        acc_ref[hd] = alpha * acc_ref[hd] + jnp.dot(vt_ref[hd, :, pl.ds(start, tq)], pt.astype(BF16),
                                                    preferred_element_type=F32)

    for hd in range(A_HEADS):
        scores_into(0, hd)

    def body(j, carry):
        for hd in range(A_HEADS):
            softmax_pv(j, hd, None)
            scores_into(j + 1, hd)
        return carry

    lax.fori_loop(0, qi, body, 0)
    kc = lax.broadcasted_iota(jnp.int32, (tq, 2 * tq), 0) // CHUNK
    qc = (lax.broadcasted_iota(jnp.int32, (tq, 2 * tq), 1) % tq) // CHUNK
    visible = kc <= qc
    for hd in range(A_HEADS):
        softmax_pv(qi, hd, visible)
    lam = lam_ref[0:1, 0:1]
    for hd in range(A_HEADS):
        inv = 1.0 / acc_ref[hd, A_VDIM:A_VDIM + 1, :]
        acc = acc_ref[hd, 0:A_VDIM, :]
        ot = acc[:, :tq] * inv[:, :tq] - (lam * inv[:, tq:]) * acc[:, tq:]
        ot = ot * lax.rsqrt(jnp.mean(ot * ot, axis=0, keepdims=True) + EPS) * (sg_ref[...] * (1.0 - lam_init))
        o_ref[:, hd * A_VDIM:(hd + 1) * A_VDIM] = ot.T.astype(o_ref.dtype)


def _attn_prompt(q, k, vt, lam, sg, lam_init, *, bsz, seq, tq):
    q, k = (t.reshape(bsz, seq, A_WIDTH) for t in (q, k))
    qspec = pl.BlockSpec((None, tq, A_WIDTH), lambda b, i: (b, i, 0))
    kspec = pl.BlockSpec((None, seq, A_WIDTH), lambda b, i: (b, 0, 0))
    vspec = pl.BlockSpec((None, A_HEADS, VT_ROWS, seq), lambda b, i: (b, 0, 0, 0))
    out = pl.pallas_call(
        functools.partial(_attn_prompt_kernel, tq=tq, lam_init=lam_init),
        grid=(bsz, seq // tq),
        in_specs=[qspec, kspec, vspec,
                  pl.BlockSpec((8, A_VDIM), lambda b, i: (0, 0)),
                  pl.BlockSpec((A_VDIM, 1), lambda b, i: (0, 0))],
        out_specs=qspec,
        out_shape=jax.ShapeDtypeStruct((bsz, seq, A_WIDTH), BF16),
        scratch_shapes=[pltpu.VMEM((A_HEADS, 2 * tq, A_VDIM), BF16),
                        pltpu.VMEM((A_HEADS, tq, 2 * tq), F32),
                        pltpu.VMEM((A_HEADS, 1, 2 * tq), F32),
                        pltpu.VMEM((A_HEADS, VT_ROWS, 2 * tq), F32)],
        compiler_params=pltpu.CompilerParams(dimension_semantics=("arbitrary",) * 2, vmem_limit_bytes=VMEM_LIMIT),
        name="attn_prompt",
    )(q, k, vt, lam, sg.reshape(A_VDIM, 1))
    return out.reshape(bsz * seq, A_WIDTH)


def _attn_sample_kernel(q_ref, kc_ref, vc_ref, kn_ref, vn_ref, lam_ref, sg_ref, o_ref, *, tq, past, lam_init):
    for hd in range(A_HEADS):
        hs = slice(hd * A_VDIM, (hd + 1) * A_VDIM)
        qs = _stack_q(q_ref[:, hs])
        sc = _qk(qs, kc_ref[pl.ds(hd, past, stride=A_HEADS), :].astype(BF16))
        sn = _qk(qs, kn_ref[:, hs])
        m = jnp.maximum(jnp.max(sc, axis=-1, keepdims=True), jnp.max(sn, axis=-1, keepdims=True))
        pc = jnp.exp2(sc - m)
        pn = jnp.exp2(sn - m)
        l = jnp.sum(pc, axis=-1, keepdims=True) + jnp.sum(pn, axis=-1, keepdims=True)
        vc = vc_ref[pl.ds(hd, past, stride=A_HEADS), :].astype(BF16)
        acc = (jnp.dot(pc.astype(BF16), vc, preferred_element_type=F32)
               + jnp.dot(pn.astype(BF16), vn_ref[:, hs], preferred_element_type=F32))
        o_ref[:, hs] = _attn_finish(acc, l, lam_ref, sg_ref, tq, lam_init).astype(o_ref.dtype)


def _attn_sample(q, k, v, cache_k, cache_v, layer, lam, sg, lam_init, *, bsz, seq):
    past = cache_k.shape[2]
    q, k, v = (t.reshape(bsz, seq, A_WIDTH) for t in (q, k, v))
    ck = cache_k.reshape(DEPTH, bsz, past * A_HEADS, A_VDIM)
    cv = cache_v.reshape(DEPTH, bsz, past * A_HEADS, A_VDIM)
    nspec = pl.BlockSpec((None, seq, A_WIDTH), lambda b: (b, 0, 0))
    cspec = pl.BlockSpec((None, None, past * A_HEADS, A_VDIM), lambda b: (layer, b, 0, 0))
    out = pl.pallas_call(
        functools.partial(_attn_sample_kernel, tq=seq, past=past, lam_init=lam_init),
        grid=(bsz,),
        in_specs=[nspec, cspec, cspec, nspec, nspec,
                  pl.BlockSpec((8, A_VDIM), lambda b: (0, 0)),
                  pl.BlockSpec((1, A_VDIM), lambda b: (0, 0))],
        out_specs=nspec,
        out_shape=jax.ShapeDtypeStruct((bsz, seq, A_WIDTH), BF16),
        compiler_params=pltpu.CompilerParams(dimension_semantics=("arbitrary",), vmem_limit_bytes=VMEM_LIMIT),
        name="attn_sample",
    )(q, ck, cv, k, v, lam, sg.reshape(1, A_VDIM))
    return out.reshape(bsz * seq, A_WIDTH)


def _outproj_kernel(x_ref, ya_ref, u_ref, uh_ref, cv_ref, cvh_ref, bg_ref, wp_ref, ps_ref, cw_ref, wo_ref,
                    o_ref, uext_ref, cext_ref, *, tm, tps, pos_off, zero_first):
    it = pl.program_id(0) % tps
    uh, cvh = uh_ref[...], cvh_ref[...]
    if zero_first:
        keep = (it > 0).astype(F32)
        uh, cvh = uh * keep, cvh * keep
    cur = u_ref[...]
    uext_ref[0:HALO, :] = uh
    uext_ref[HALO:, :] = cur
    cv = cv_ref[...]
    cext_ref[0:HALO, :] = cvh
    cext_ref[HALO:, :] = cv

    run = cur
    sums = {}
    for j in range(1, max(POOL_WINDOWS)):
        run = run + uext_ref[pl.ds(HALO - j, tm), :]
        if j + 1 in POOL_WINDOWS:
            sums[j + 1] = run
    pos = pos_off + it * tm + lax.broadcasted_iota(jnp.int32, (tm, 1), 0)
    lane = lax.broadcasted_iota(jnp.int32, (tm, B_WIDTH), 1)
    d = None
    for gi, w in reversed(list(enumerate(POOL_WINDOWS))):
        mean = sums[w] / jnp.minimum(pos + 1, w).astype(F32)
        d = mean if d is None else jnp.where(lane < (gi + 1) * POOL_GROUP, mean, d)
    d = d - cur
    y_pool = jnp.dot(d.astype(BF16), wp_ref[...], preferred_element_type=F32) * ps_ref[...]

    cw = cw_ref[...]
    conv = cw[0:1, :] * cext_ref[pl.ds(HALO - 2, tm), :] + cw[1:2, :] * cext_ref[pl.ds(HALO - 1, tm), :]
    conv = conv + cw[2:3, :] * cv
    y_conv = bg_ref[...] * conv

    mixed = jnp.concatenate([ya_ref[...], y_pool.astype(BF16), y_conv.astype(BF16)], axis=-1)
    o_ref[...] = x_ref[...] + jnp.dot(mixed, wo_ref[...], preferred_element_type=F32)


def _outproj(x, ya, u, cv, bg, u_halo, cv_halo, halo_map, wp, ps, cw, wo, *, tm, seq, pos_off, zero_first):
    n, d = x.shape
    tps = seq // tm
    row = lambda w: pl.BlockSpec((tm, w), lambda i: (i, 0))
    halo = pl.BlockSpec((HALO, B_WIDTH), halo_map)
    return pl.pallas_call(
        functools.partial(_outproj_kernel, tm=tm, tps=tps, pos_off=pos_off, zero_first=zero_first),
        grid=(n // tm,),
        in_specs=[row(d), row(A_WIDTH), row(B_WIDTH), halo, row(C_WIDTH), halo, row(C_WIDTH),
                  _const_spec(wp.shape), _const_spec((1, B_WIDTH)), _const_spec(cw.shape), _const_spec(wo.shape)],
        out_specs=row(d),
        out_shape=jax.ShapeDtypeStruct((n, d), F32),
        scratch_shapes=[pltpu.VMEM((HALO + tm, B_WIDTH), F32), pltpu.VMEM((HALO + tm, C_WIDTH), F32)],
        compiler_params=pltpu.CompilerParams(dimension_semantics=("arbitrary",), vmem_limit_bytes=VMEM_LIMIT),
        name="outproj",
    )(x, ya, u, u_halo, cv, cv_halo, bg, wp, ps.reshape(1, B_WIDTH), cw, wo)


def _block_diag(w):
    g, c, _ = w.shape
    eye = jnp.eye(g, dtype=w.dtype)
    return (eye[:, None, :, None] * w[:, :, None, :]).reshape(g * c, g * c)


def _front_pad(state, rows):
    b, r, c = state.shape
    return jnp.pad(state, ((0, 0), (rows - r, 0), (0, 0))).reshape(b * rows, c)


def _trunk(x, pos0, cache, p, *, tm_ffn, tm_mix, tq):
    bsz, seq, d = x.shape
    n = bsz * seq
    x = x.reshape(n, d)
    tables = _rope_tables(pos0 + np.arange(seq))
    kv_leaves, new_pool, new_conv = None, [], []
    for l in range(DEPTH):
        lam_init = 0.8 - 0.6 * math.exp(-0.3 * l)
        x = _ffn(x, p['norm_ffn1'][l], p['ffn1_gate'][l], p['ffn1_up'][l], p['ffn1_down'][l], tm=tm_ffn)
        lams = [p[name][l] for name in ('lambda_q1', 'lambda_k1', 'lambda_q2', 'lambda_k2')]
        q, k_all, kb, v_all, vb, u, cv, bg, lam = _inproj(x, p['norm_mix'][l], p['w_in'][l], tables, lams, lam_init,
                                                          l, kv_leaves, tm=tm_mix, seq=seq,
                                                          v_transposed=cache is None)
        kv_leaves = (k_all, v_all)
        if cache is None:
            ya = _attn_prompt(q, kb, vb, lam, p['subln'][l], lam_init, bsz=bsz, seq=seq, tq=tq)
            u_halo, cv_halo = u, cv
            per = tm_mix // HALO
            halo_map = lambda i, per=per: (jnp.maximum(i * per - 1, 0), 0)
            pool_ext, conv_ext = u.reshape(bsz, seq, B_WIDTH), cv.reshape(bsz, seq, C_WIDTH)
        else:
            cache_k, cache_v, state_pool, state_conv = cache
            ya = _attn_sample(q, kb, vb, cache_k, cache_v, l, lam, p['subln'][l], lam_init, bsz=bsz, seq=seq)
            u_halo = _front_pad(state_pool[l], HALO)
            cv_halo = _front_pad(state_conv[l], HALO)
            halo_map = lambda i: (i, 0)
            pool_ext = jnp.concatenate([state_pool[l], u.reshape(bsz, seq, B_WIDTH)], axis=1)
            conv_ext = jnp.concatenate([state_conv[l], cv.reshape(bsz, seq, C_WIDTH)], axis=1)
        x = _outproj(x, ya, u, cv, bg, u_halo, cv_halo, halo_map,
                     p['pool_w_bd'][l], p['pool_scale'][l], p['conv_w'][l], p['w_out'][l],
                     tm=tm_mix, seq=seq, pos_off=pos0, zero_first=cache is None)
        x = _ffn(x, p['norm_ffn2'][l], p['ffn2_gate'][l], p['ffn2_up'][l], p['ffn2_down'][l],
                 p['final_norm'] if l == DEPTH - 1 else None, tm=tm_ffn)
        new_pool.append(pool_ext[:, -POOL_HIST:])
        new_conv.append(conv_ext[:, -(CONV_W - 1):])
    new_k, new_v = (t.reshape(DEPTH, bsz, seq, A_HEADS, A_VDIM) for t in kv_leaves)
    return (x.reshape(bsz, seq, d), new_k, new_v, jnp.stack(new_pool), jnp.stack(new_conv))


def kernel(x_prompt, x_sample, cache_k, cache_v, state_pool, state_conv, norm_ffn1, ffn1_gate, ffn1_up, ffn1_down,
           norm_mix, w_in, lambda_q1, lambda_k1, lambda_q2, lambda_k2, subln, pool_w, pool_scale, conv_w, w_out,
           norm_ffn2, ffn2_gate, ffn2_up, ffn2_down, final_norm):
    p = {
        'norm_ffn1': norm_ffn1, 'norm_mix': norm_mix, 'norm_ffn2': norm_ffn2, 'final_norm': final_norm,
        'lambda_q1': lambda_q1, 'lambda_k1': lambda_k1, 'lambda_q2': lambda_q2, 'lambda_k2': lambda_k2,
        'subln': subln, 'pool_scale': pool_scale, 'conv_w': conv_w,
        'ffn1_gate': ffn1_gate.astype(BF16), 'ffn1_up': ffn1_up.astype(BF16), 'ffn1_down': ffn1_down.astype(BF16),
        'ffn2_gate': ffn2_gate.astype(BF16), 'ffn2_up': ffn2_up.astype(BF16), 'ffn2_down': ffn2_down.astype(BF16),
        'w_in': w_in.astype(BF16), 'w_out': w_out.astype(BF16),
        'pool_w_bd': jnp.stack([_block_diag(pool_w[l]) for l in range(DEPTH)]).astype(BF16),
    }
    y_p, k_p, v_p, pool_p, conv_p = _trunk(x_prompt, 0, None, p, tm_ffn=512, tm_mix=512, tq=256)
    y_s, k_s, v_s, pool_s, conv_s = _trunk(x_sample, PAST_LEN, (cache_k, cache_v, state_pool, state_conv), p,
                                           tm_ffn=256, tm_mix=32, tq=None)
    return (y_p, y_s, k_p, v_p, pool_p, conv_p, k_s, v_s, pool_s, conv_s)
```

```python
import functools
import math

import numpy as np
import jax
import jax.numpy as jnp
from jax import lax
from jax.experimental import pallas as pl
from jax.experimental.pallas import tpu as pltpu

F32 = jnp.float32
BF16 = jnp.bfloat16

D_MODEL = 1024
DEPTH = 2
PAST_LEN = 4096
CHUNK = 64
A_WIDTH = 512
A_HEAD_DIM = 64
A_HEADS = 4
A_VDIM = 128
ATTN_SCALE = A_HEAD_DIM ** -0.5
B_WIDTH = 256
POOL_WINDOWS = (2, 4, 8, 16)
POOL_GROUP = 64
POOL_HIST = 15
C_WIDTH = 256
CONV_W = 3
D_FF = 2816
ROPE_THETA = 10000.0
EPS = 1e-6
NEG_INF = -1e30
D_IN = 3 * A_WIDTH + B_WIDTH + 3 * C_WIDTH

LOG2E = math.log2(math.e)
BF16_ROWS = 16
VT_ROWS = A_VDIM + BF16_ROWS
HALO = 16
FF_CHUNK = 256
VMEM_LIMIT = 56 * 1024 * 1024


def _rms(x, g):
    return x * lax.rsqrt(jnp.mean(x * x, axis=-1, keepdims=True) + EPS) * g


def _const_spec(shape):
    return pl.BlockSpec(shape, lambda *_: (0,) * len(shape), pipeline_mode=pl.Buffered(1))


def _layer_spec(w, layer):
    tail = w.shape[1:]
    return pl.BlockSpec((None,) + tail, lambda *_: (layer,) + (0,) * len(tail), pipeline_mode=pl.Buffered(1))


def _ffn_kernel(x_ref, g_ref, wg_ref, wu_ref, wd_ref, *rest, final_norm):
    fg_ref = rest[0] if final_norm else None
    rest[-1][...] = _ffn_body(x_ref[...], g_ref, wg_ref, wu_ref, wd_ref, fg_ref)


def _ffn_body(x, g_ref, wg_ref, wu_ref, wd_ref, fg_ref):
    h = _rms(x, g_ref[...]).astype(BF16)
    acc = jnp.zeros(x.shape, F32)
    for c in range(D_FF // FF_CHUNK):
        sl = slice(c * FF_CHUNK, (c + 1) * FF_CHUNK)
        gate = jnp.dot(h, wg_ref[:, sl].astype(BF16), preferred_element_type=F32)
        up = jnp.dot(h, wu_ref[:, sl].astype(BF16), preferred_element_type=F32)
        act = (gate * (1.0 / (1.0 + jnp.exp(-gate))) * up).astype(BF16)
        acc = acc + jnp.dot(act, wd_ref[sl, :].astype(BF16), preferred_element_type=F32)
    y = x + 0.5 * acc
    return y if fg_ref is None else _rms(y, fg_ref[...])


def _ffn(x, g, wg, wu, wd, layer, final_g=None, *, tm):
    n, d = x.shape
    assert n % tm == 0
    row = pl.BlockSpec((tm, d), lambda i: (i, 0))
    in_specs = [row, _const_spec((1, d)), _layer_spec(wg, layer), _layer_spec(wu, layer), _layer_spec(wd, layer)]
    args = [x, g.reshape(1, d), wg, wu, wd]
    if final_g is not None:
        in_specs.append(_const_spec((1, d)))
        args.append(final_g.reshape(1, d))
    return pl.pallas_call(
        functools.partial(_ffn_kernel, final_norm=final_g is not None),
        grid=(n // tm,),
        in_specs=in_specs,
        out_specs=row,
        out_shape=jax.ShapeDtypeStruct((n, d), F32),
        compiler_params=pltpu.CompilerParams(dimension_semantics=("arbitrary",), vmem_limit_bytes=VMEM_LIMIT),
        name="ffn",
    )(*args)


def _inproj_kernel(x_ref, g_ref, w_ref, cos_ref, sa_ref, sb_ref, lq1_ref, lk1_ref, lq2_ref, lk2_ref, *rest,
                   lam_init, has_prev, v_transposed):
    q_ref, k_ref, kb_ref, v_ref, vb_ref, u_ref, cv_ref, bg_ref, lam_ref = rest[2:] if has_prev else rest
    tm = x_ref.shape[0]

    @pl.when(pl.program_id(0) == 0)
    def _():
        a = jnp.sum(lq1_ref[...] * lk1_ref[...], axis=-1, keepdims=True)
        b = jnp.sum(lq2_ref[...] * lk2_ref[...], axis=-1, keepdims=True)
        lam_ref[...] = jnp.broadcast_to(jnp.exp(a) - jnp.exp(b) + lam_init, lam_ref.shape)

    h = _rms(x_ref[...], g_ref[...]).astype(BF16)
    z = jnp.dot(h, w_ref[...], preferred_element_type=F32)
    cos, sa, sb = cos_ref[...], sa_ref[...], sb_ref[...]

    def rope(t):
        return t * cos + pltpu.roll(t, 96, 1) * sa + pltpu.roll(t, 32, 1) * sb

    for hd in range(A_HEADS):
        sl = slice(hd * A_VDIM, (hd + 1) * A_VDIM)
        qh = rope(z[:, hd * A_VDIM:(hd + 1) * A_VDIM])
        q_ref[:, sl] = (qh * (ATTN_SCALE * LOG2E)).astype(BF16)
        kh = rope(z[:, A_WIDTH + hd * A_VDIM:A_WIDTH + (hd + 1) * A_VDIM])
        k_ref[pl.ds(hd, tm, stride=A_HEADS), :] = kh
        kb_ref[:, sl] = kh.astype(BF16)
    v = z[:, 2 * A_WIDTH:3 * A_WIDTH]
    for hd in range(A_HEADS):
        v_ref[pl.ds(hd, tm, stride=A_HEADS), :] = v[:, hd * A_VDIM:(hd + 1) * A_VDIM]
    if v_transposed:
        vt = v.T.astype(BF16)
        for hd in range(A_HEADS):
            vb_ref[hd, 0:A_VDIM, :] = vt[hd * A_VDIM:(hd + 1) * A_VDIM, :]
            vb_ref[hd, A_VDIM:VT_ROWS, :] = jnp.ones((BF16_ROWS, tm), BF16)
    else:
        vb_ref[...] = v.astype(BF16)
    o = 3 * A_WIDTH
    u_ref[...] = z[:, o:o + B_WIDTH]
    hc = z[:, o + B_WIDTH:o + B_WIDTH + C_WIDTH]
    bg_ref[...] = z[:, o + B_WIDTH + C_WIDTH:o + B_WIDTH + 2 * C_WIDTH]
    cg = z[:, o + B_WIDTH + 2 * C_WIDTH:o + B_WIDTH + 3 * C_WIDTH]
    cv_ref[...] = cg * hc


def _rope_tables(pos):
    half = A_HEAD_DIM // 2
    inv = np.power(ROPE_THETA, -np.arange(half, dtype=np.float64) / half)
    ang = pos.astype(np.float64)[:, None] * inv[None, :]
    cos, sin, zero = np.cos(ang), np.sin(ang), np.zeros_like(ang)
    cos_t = np.concatenate([cos, cos, cos, cos], axis=1)
    sa = np.concatenate([-sin, zero, -sin, zero], axis=1)
    sb = np.concatenate([zero, sin, zero, sin], axis=1)
    return [jnp.asarray(t, F32) for t in (cos_t, sa, sb)]


def _inproj(x, g, w_in, tables, lams, lam_init, layer, kv_prev, *, tm, seq, v_transposed):
    n, d = x.shape
    tps = seq // tm
    if v_transposed:
        vb_shape = jax.ShapeDtypeStruct((n // seq, A_HEADS, VT_ROWS, seq), BF16)
        vb_spec = pl.BlockSpec((None, A_HEADS, VT_ROWS, tm), lambda i: (i // tps, 0, 0, i % tps))
    else:
        vb_shape = jax.ShapeDtypeStruct((n, A_WIDTH), BF16)
        vb_spec = pl.BlockSpec((tm, A_WIDTH), lambda i: (i, 0))
    row = lambda w: pl.BlockSpec((tm, w), lambda i: (i, 0))
    tab = pl.BlockSpec((tm, A_VDIM), lambda i: (i % tps, 0))
    lam_spec = _const_spec((1, A_HEAD_DIM))
    leaf = jax.ShapeDtypeStruct((DEPTH, n * A_HEADS, A_VDIM), F32)
    leaf_spec = pl.BlockSpec((None, tm * A_HEADS, A_VDIM), lambda i: (layer, i, 0))
    out_shape = [
        jax.ShapeDtypeStruct((n, A_WIDTH), BF16),
        leaf,
        jax.ShapeDtypeStruct((n, A_WIDTH), BF16),
        leaf,
        vb_shape,
        jax.ShapeDtypeStruct((n, B_WIDTH), F32),
        jax.ShapeDtypeStruct((n, C_WIDTH), F32),
        jax.ShapeDtypeStruct((n, C_WIDTH), F32),
        jax.ShapeDtypeStruct((8, A_VDIM), F32),
    ]
    out_specs = [row(A_WIDTH), leaf_spec, row(A_WIDTH), leaf_spec, vb_spec, row(B_WIDTH), row(C_WIDTH),
                 row(C_WIDTH), pl.BlockSpec((8, A_VDIM), lambda i: (0, 0))]
    in_specs = [row(d), _const_spec((1, d)), _const_spec(w_in.shape), tab, tab, tab] + [lam_spec] * 4
    args = [x, g.reshape(1, d), w_in, *tables, *[v.reshape(1, A_HEAD_DIM) for v in lams]]
    aliases = {}
    if kv_prev is not None:
        aliases = {len(args): 1, len(args) + 1: 3}
        in_specs += [pl.BlockSpec(memory_space=pl.ANY)] * 2
        args += list(kv_prev)
    return pl.pallas_call(
        functools.partial(_inproj_kernel, lam_init=lam_init, has_prev=kv_prev is not None,
                          v_transposed=v_transposed),
        grid=(n // tm,),
        in_specs=in_specs,
        out_specs=out_specs,
        out_shape=out_shape,
        input_output_aliases=aliases,
        compiler_params=pltpu.CompilerParams(dimension_semantics=("arbitrary",), vmem_limit_bytes=VMEM_LIMIT),
        name="inproj",
    )(*args)


def _stack_q(q):
    lane = lax.broadcasted_iota(jnp.int32, q.shape, 1)
    zero = jnp.zeros_like(q)
    return jnp.concatenate([jnp.where(lane < A_HEAD_DIM, q, zero), jnp.where(lane >= A_HEAD_DIM, q, zero)], axis=0)


def _qk(qs, k):
    return lax.dot_general(qs, k, (((1,), (1,)), ((), ())), preferred_element_type=F32)


def _attn_finish(acc, l, lam_ref, sg_ref, tq, lam_init):
    o = acc[:tq] / l[:tq] - lam_ref[0:1, :] * (acc[tq:] / l[tq:])
    return _rms(o, sg_ref[...]) * (1.0 - lam_init)


def _attn_prompt_kernel(q_ref, k_ref, vt_ref, lam_ref, sg_ref, o_ref, qs_ref, s_ref, m_ref, acc_ref, *, tq, lam_init):
    qi = pl.program_id(1)
    for hd in range(A_HEADS):
        qs_ref[hd] = _stack_q(q_ref[:, hd * A_VDIM:(hd + 1) * A_VDIM])
    m_ref[...] = jnp.full(m_ref.shape, NEG_INF, F32)
    acc_ref[...] = jnp.zeros(acc_ref.shape, F32)

    def scores_into(j, hd):
        start = pl.multiple_of(j * tq, tq)
        s_ref[hd] = _qk(k_ref[pl.ds(start, tq), hd * A_VDIM:(hd + 1) * A_VDIM], qs_ref[hd])

    def softmax_pv(j, hd, visible):
        start = pl.multiple_of(j * tq, tq)
        st = s_ref[hd]
        if visible is not None:
            st = jnp.where(visible, st, NEG_INF)
        m_old = m_ref[hd]
        m_new = jnp.maximum(m_old, jnp.max(st, axis=0, keepdims=True))
        alpha = jnp.exp2(m_old - m_new)
        pt = jnp.exp2(st - m_new)
        m_ref[hd] = m_new
        acc_ref[hd] = alpha * acc_ref[hd] + jnp.dot(vt_ref[hd, :, pl.ds(start, tq)], pt.astype(BF16),
                                                    preferred_element_type=F32)

    for hd in range(A_HEADS):
        scores_into(0, hd)

    def body(j, carry):
        for hd in range(A_HEADS):
            softmax_pv(j, hd, None)
            scores_into(j + 1, hd)
        return carry

    lax.fori_loop(0, qi, body, 0)
    kc = lax.broadcasted_iota(jnp.int32, (tq, 2 * tq), 0) // CHUNK
    qc = (lax.broadcasted_iota(jnp.int32, (tq, 2 * tq), 1) % tq) // CHUNK
    visible = kc <= qc
    for hd in range(A_HEADS):
        softmax_pv(qi, hd, visible)
    lam = lam_ref[0:1, 0:1]
    for hd in range(A_HEADS):
        inv = 1.0 / acc_ref[hd, A_VDIM:A_VDIM + 1, :]
        acc = acc_ref[hd, 0:A_VDIM, :]
        ot = acc[:, :tq] * inv[:, :tq] - (lam * inv[:, tq:]) * acc[:, tq:]
        ot = ot * lax.rsqrt(jnp.mean(ot * ot, axis=0, keepdims=True) + EPS) * (sg_ref[...] * (1.0 - lam_init))
        o_ref[:, hd * A_VDIM:(hd + 1) * A_VDIM] = ot.T.astype(o_ref.dtype)


def _attn_prompt(q, k, vt, lam, sg, lam_init, *, bsz, seq, tq):
    q, k = (t.reshape(bsz, seq, A_WIDTH) for t in (q, k))
    qspec = pl.BlockSpec((None, tq, A_WIDTH), lambda b, i: (b, i, 0))
    kspec = pl.BlockSpec((None, seq, A_WIDTH), lambda b, i: (b, 0, 0))
    vspec = pl.BlockSpec((None, A_HEADS, VT_ROWS, seq), lambda b, i: (b, 0, 0, 0))
    out = pl.pallas_call(
        functools.partial(_attn_prompt_kernel, tq=tq, lam_init=lam_init),
        grid=(bsz, seq // tq),
        in_specs=[qspec, kspec, vspec,
                  pl.BlockSpec((8, A_VDIM), lambda b, i: (0, 0)),
                  pl.BlockSpec((A_VDIM, 1), lambda b, i: (0, 0))],
        out_specs=qspec,
        out_shape=jax.ShapeDtypeStruct((bsz, seq, A_WIDTH), BF16),
        scratch_shapes=[pltpu.VMEM((A_HEADS, 2 * tq, A_VDIM), BF16),
                        pltpu.VMEM((A_HEADS, tq, 2 * tq), F32),
                        pltpu.VMEM((A_HEADS, 1, 2 * tq), F32),
                        pltpu.VMEM((A_HEADS, VT_ROWS, 2 * tq), F32)],
        compiler_params=pltpu.CompilerParams(dimension_semantics=("arbitrary",) * 2, vmem_limit_bytes=VMEM_LIMIT),
        name="attn_prompt",
    )(q, k, vt, lam, sg.reshape(A_VDIM, 1))
    return out.reshape(bsz * seq, A_WIDTH)


def _attn_sample_kernel(q_ref, kc_ref, vc_ref, kn_ref, vn_ref, lam_ref, sg_ref, o_ref, *, tq, past, lam_init):
    for hd in range(A_HEADS):
        hs = slice(hd * A_VDIM, (hd + 1) * A_VDIM)
        qs = _stack_q(q_ref[:, hs])
        sc = _qk(qs, kc_ref[pl.ds(hd, past, stride=A_HEADS), :].astype(BF16))
        sn = _qk(qs, kn_ref[:, hs])
        m = jnp.maximum(jnp.max(sc, axis=-1, keepdims=True), jnp.max(sn, axis=-1, keepdims=True))
        pc = jnp.exp2(sc - m)
        pn = jnp.exp2(sn - m)
        l = jnp.sum(pc, axis=-1, keepdims=True) + jnp.sum(pn, axis=-1, keepdims=True)
        vc = vc_ref[pl.ds(hd, past, stride=A_HEADS), :].astype(BF16)
        acc = (jnp.dot(pc.astype(BF16), vc, preferred_element_type=F32)
               + jnp.dot(pn.astype(BF16), vn_ref[:, hs], preferred_element_type=F32))
        o_ref[:, hs] = _attn_finish(acc, l, lam_ref, sg_ref, tq, lam_init).astype(o_ref.dtype)


def _attn_sample(q, k, v, cache_k, cache_v, layer, lam, sg, lam_init, *, bsz, seq):
    past = cache_k.shape[2]
    q, k, v = (t.reshape(bsz, seq, A_WIDTH) for t in (q, k, v))
    ck = cache_k.reshape(DEPTH, bsz, past * A_HEADS, A_VDIM)
    cv = cache_v.reshape(DEPTH, bsz, past * A_HEADS, A_VDIM)
    nspec = pl.BlockSpec((None, seq, A_WIDTH), lambda b: (b, 0, 0))
    cspec = pl.BlockSpec((None, None, past * A_HEADS, A_VDIM), lambda b: (layer, b, 0, 0))
    out = pl.pallas_call(
        functools.partial(_attn_sample_kernel, tq=seq, past=past, lam_init=lam_init),
        grid=(bsz,),
        in_specs=[nspec, cspec, cspec, nspec, nspec,
                  pl.BlockSpec((8, A_VDIM), lambda b: (0, 0)),
                  pl.BlockSpec((1, A_VDIM), lambda b: (0, 0))],
        out_specs=nspec,
        out_shape=jax.ShapeDtypeStruct((bsz, seq, A_WIDTH), BF16),
        compiler_params=pltpu.CompilerParams(dimension_semantics=("arbitrary",), vmem_limit_bytes=VMEM_LIMIT),
        name="attn_sample",
    )(q, ck, cv, k, v, lam, sg.reshape(1, A_VDIM))
    return out.reshape(bsz * seq, A_WIDTH)


def _outproj_kernel(x_ref, ya_ref, u_ref, uh_ref, cv_ref, cvh_ref, bg_ref, wp_ref, ps_ref, cw_ref, wo_ref,
                    *rest, tm, tps, pos_off, zero_first, n_ffn):
    ffn_refs, (o_ref, uext_ref, cext_ref) = rest[:n_ffn], rest[n_ffn:]
    it = pl.program_id(0) % tps
    uh, cvh = uh_ref[...], cvh_ref[...]
    if zero_first:
        keep = (it > 0).astype(F32)
        uh, cvh = uh * keep, cvh * keep
    cur = u_ref[...]
    uext_ref[0:HALO, :] = uh
    uext_ref[HALO:, :] = cur
    cv = cv_ref[...]
    cext_ref[0:HALO, :] = cvh
    cext_ref[HALO:, :] = cv

    run = cur
    sums = {}
    for j in range(1, max(POOL_WINDOWS)):
        run = run + uext_ref[pl.ds(HALO - j, tm), :]
        if j + 1 in POOL_WINDOWS:
            sums[j + 1] = run
    pos = pos_off + it * tm + lax.broadcasted_iota(jnp.int32, (tm, 1), 0)
    lane = lax.broadcasted_iota(jnp.int32, (tm, B_WIDTH), 1)
    d = None
    for gi, w in reversed(list(enumerate(POOL_WINDOWS))):
        mean = sums[w] / jnp.minimum(pos + 1, w).astype(F32)
        d = mean if d is None else jnp.where(lane < (gi + 1) * POOL_GROUP, mean, d)
    d = d - cur
    y_pool = jnp.dot(d.astype(BF16), wp_ref[...], preferred_element_type=F32) * ps_ref[...]

    cw = cw_ref[...]
    conv = cw[0:1, :] * cext_ref[pl.ds(HALO - 2, tm), :] + cw[1:2, :] * cext_ref[pl.ds(HALO - 1, tm), :]
    conv = conv + cw[2:3, :] * cv
    y_conv = bg_ref[...] * conv

    mixed = jnp.concatenate([ya_ref[...], y_pool.astype(BF16), y_conv.astype(BF16)], axis=-1)
    y = x_ref[...] + jnp.dot(mixed, wo_ref[...], preferred_element_type=F32)
    if ffn_refs:
        y = _ffn_body(y, *ffn_refs[:4], ffn_refs[4] if n_ffn == 5 else None)
    o_ref[...] = y


def _outproj(x, ya, u, cv, bg, u_halo, cv_halo, halo_map, wp, ps, cw, wo, ffn=None, *, tm, seq, pos_off,
             zero_first):
    n, d = x.shape
    tps = seq // tm
    row = lambda w: pl.BlockSpec((tm, w), lambda i: (i, 0))
    halo = pl.BlockSpec((HALO, B_WIDTH), halo_map)
    in_specs = [row(d), row(A_WIDTH), row(B_WIDTH), halo, row(C_WIDTH), halo, row(C_WIDTH),
                _const_spec(wp.shape), _const_spec((1, B_WIDTH)), _const_spec(cw.shape), _const_spec(wo.shape)]
    args = [x, ya, u, u_halo, cv, cv_halo, bg, wp, ps.reshape(1, B_WIDTH), cw, wo]
    if ffn is not None:
        g, wg, wu, wd, layer, final_g = ffn
        in_specs += [_const_spec((1, d)), _layer_spec(wg, layer), _layer_spec(wu, layer), _layer_spec(wd, layer)]
        args += [g.reshape(1, d), wg, wu, wd]
        if final_g is not None:
            in_specs.append(_const_spec((1, d)))
            args.append(final_g.reshape(1, d))
    return pl.pallas_call(
        functools.partial(_outproj_kernel, tm=tm, tps=tps, pos_off=pos_off, zero_first=zero_first,
                          n_ffn=len(args) - 11),
        grid=(n // tm,),
        in_specs=in_specs,
        out_specs=row(d),
        out_shape=jax.ShapeDtypeStruct((n, d), F32),
        scratch_shapes=[pltpu.VMEM((HALO + tm, B_WIDTH), F32), pltpu.VMEM((HALO + tm, C_WIDTH), F32)],
        compiler_params=pltpu.CompilerParams(dimension_semantics=("arbitrary",), vmem_limit_bytes=VMEM_LIMIT),
        name="outproj_ffn" if ffn is not None else "outproj",
    )(*args)


def _block_diag(w):
    g, c, _ = w.shape
    eye = jnp.eye(g, dtype=w.dtype)
    return (eye[:, None, :, None] * w[:, :, None, :]).reshape(g * c, g * c)


def _front_pad(state, rows):
    b, r, c = state.shape
    return jnp.pad(state, ((0, 0), (rows - r, 0), (0, 0))).reshape(b * rows, c)


def _trunk(x, pos0, cache, p, *, tm_ffn, tm_mix, tq):
    bsz, seq, d = x.shape
    n = bsz * seq
    x = x.reshape(n, d)
    tables = _rope_tables(pos0 + np.arange(seq))
    kv_leaves, new_pool, new_conv = None, [], []
    for l in range(DEPTH):
        lam_init = 0.8 - 0.6 * math.exp(-0.3 * l)
        x = _ffn(x, p['norm_ffn1'][l], p['ffn1_gate'], p['ffn1_up'], p['ffn1_down'], l, tm=tm_ffn)
        lams = [p[name][l] for name in ('lambda_q1', 'lambda_k1', 'lambda_q2', 'lambda_k2')]
        q, k_all, kb, v_all, vb, u, cv, bg, lam = _inproj(x, p['norm_mix'][l], p['w_in'][l], tables, lams, lam_init,
                                                          l, kv_leaves, tm=tm_mix, seq=seq,
                                                          v_transposed=cache is None)
        kv_leaves = (k_all, v_all)
        if cache is None:
            ya = _attn_prompt(q, kb, vb, lam, p['subln'][l], lam_init, bsz=bsz, seq=seq, tq=tq)
            u_halo, cv_halo = u, cv
            per = tm_mix // HALO
            halo_map = lambda i, per=per: (jnp.maximum(i * per - 1, 0), 0)
            pool_ext, conv_ext = u.reshape(bsz, seq, B_WIDTH), cv.reshape(bsz, seq, C_WIDTH)
        else:
            cache_k, cache_v, state_pool, state_conv = cache
            ya = _attn_sample(q, kb, vb, cache_k, cache_v, l, lam, p['subln'][l], lam_init, bsz=bsz, seq=seq)
            u_halo = _front_pad(state_pool[l], HALO)
            cv_halo = _front_pad(state_conv[l], HALO)
            halo_map = lambda i: (i, 0)
            pool_ext = jnp.concatenate([state_pool[l], u.reshape(bsz, seq, B_WIDTH)], axis=1)
            conv_ext = jnp.concatenate([state_conv[l], cv.reshape(bsz, seq, C_WIDTH)], axis=1)
        final_g = p['final_norm'] if l == DEPTH - 1 else None
        ffn2 = (p['norm_ffn2'][l], p['ffn2_gate'], p['ffn2_up'], p['ffn2_down'], l, final_g)
        fuse = tm_mix == tm_ffn
        x = _outproj(x, ya, u, cv, bg, u_halo, cv_halo, halo_map,
                     p['pool_w_bd'][l], p['pool_scale'][l], p['conv_w'][l], p['w_out'][l], ffn2 if fuse else None,
                     tm=tm_mix, seq=seq, pos_off=pos0, zero_first=cache is None)
        if not fuse:
            x = _ffn(x, *ffn2, tm=tm_ffn)
        new_pool.append(pool_ext[:, -POOL_HIST:])
        new_conv.append(conv_ext[:, -(CONV_W - 1):])
    new_k, new_v = (t.reshape(DEPTH, bsz, seq, A_HEADS, A_VDIM) for t in kv_leaves)
    return (x.reshape(bsz, seq, d), new_k, new_v, jnp.stack(new_pool), jnp.stack(new_conv))


def kernel(x_prompt, x_sample, cache_k, cache_v, state_pool, state_conv, norm_ffn1, ffn1_gate, ffn1_up, ffn1_down,
           norm_mix, w_in, lambda_q1, lambda_k1, lambda_q2, lambda_k2, subln, pool_w, pool_scale, conv_w, w_out,
           norm_ffn2, ffn2_gate, ffn2_up, ffn2_down, final_norm):
    p = {
        'norm_ffn1': norm_ffn1, 'norm_mix': norm_mix, 'norm_ffn2': norm_ffn2, 'final_norm': final_norm,
        'lambda_q1': lambda_q1, 'lambda_k1': lambda_k1, 'lambda_q2': lambda_q2, 'lambda_k2': lambda_k2,
        'subln': subln, 'pool_scale': pool_scale, 'conv_w': conv_w,
        'ffn1_gate': ffn1_gate, 'ffn1_up': ffn1_up, 'ffn1_down': ffn1_down,
        'ffn2_gate': ffn2_gate, 'ffn2_up': ffn2_up, 'ffn2_down': ffn2_down,
        'w_in': w_in.astype(BF16), 'w_out': w_out.astype(BF16),
        'pool_w_bd': jnp.stack([_block_diag(pool_w[l]) for l in range(DEPTH)]).astype(BF16),
    }
    y_p, k_p, v_p, pool_p, conv_p = _trunk(x_prompt, 0, None, p, tm_ffn=512, tm_mix=512, tq=256)
    y_s, k_s, v_s, pool_s, conv_s = _trunk(x_sample, PAST_LEN, (cache_k, cache_v, state_pool, state_conv), p,
                                           tm_ffn=256, tm_mix=32, tq=None)
    return (y_p, y_s, k_p, v_p, pool_p, conv_p, k_s, v_s, pool_s, conv_s)
```

```python
import functools
import math

import numpy as np
import jax
import jax.numpy as jnp
from jax import lax
from jax.experimental import pallas as pl
from jax.experimental.pallas import tpu as pltpu

F32 = jnp.float32
BF16 = jnp.bfloat16

D_MODEL = 1024
DEPTH = 2
PAST_LEN = 4096
CHUNK = 64
A_WIDTH = 512
A_HEAD_DIM = 64
A_HEADS = 4
A_VDIM = 128
ATTN_SCALE = A_HEAD_DIM ** -0.5
B_WIDTH = 256
POOL_WINDOWS = (2, 4, 8, 16)
POOL_GROUP = 64
POOL_HIST = 15
C_WIDTH = 256
CONV_W = 3
D_FF = 2816
ROPE_THETA = 10000.0
EPS = 1e-6
NEG_INF = -1e30
D_IN = 3 * A_WIDTH + B_WIDTH + 3 * C_WIDTH

LOG2E = math.log2(math.e)
BF16_ROWS = 16
VT_ROWS = A_VDIM + BF16_ROWS
HALO = 16
FF_CHUNK = 256
VMEM_LIMIT = 56 * 1024 * 1024


def _rms(x, g):
    return x * lax.rsqrt(jnp.mean(x * x, axis=-1, keepdims=True) + EPS) * g


def _const_spec(shape):
    return pl.BlockSpec(shape, lambda *_: (0,) * len(shape), pipeline_mode=pl.Buffered(1))


def _layer_spec(w, layer):
    tail = w.shape[1:]
    return pl.BlockSpec((None,) + tail, lambda *_: (layer,) + (0,) * len(tail), pipeline_mode=pl.Buffered(1))


def _ffn_kernel(x_ref, g_ref, wg_ref, wu_ref, wd_ref, *rest, final_norm):
    fg_ref = rest[0] if final_norm else None
    rest[-1][...] = _ffn_body(x_ref[...], g_ref, wg_ref, wu_ref, wd_ref, fg_ref)


def _ffn_body(x, g_ref, wg_ref, wu_ref, wd_ref, fg_ref):
    h = _rms(x, g_ref[...]).astype(BF16)
    acc = jnp.zeros(x.shape, F32)
    for c in range(D_FF // FF_CHUNK):
        sl = slice(c * FF_CHUNK, (c + 1) * FF_CHUNK)
        gate = jnp.dot(h, wg_ref[:, sl].astype(BF16), preferred_element_type=F32)
        up = jnp.dot(h, wu_ref[:, sl].astype(BF16), preferred_element_type=F32)
        act = (gate * (1.0 / (1.0 + jnp.exp(-gate))) * up).astype(BF16)
        acc = acc + jnp.dot(act, wd_ref[sl, :].astype(BF16), preferred_element_type=F32)
    y = x + 0.5 * acc
    return y if fg_ref is None else _rms(y, fg_ref[...])


def _ffn(x, g, wg, wu, wd, layer, final_g=None, *, tm):
    n, d = x.shape
    assert n % tm == 0
    row = pl.BlockSpec((tm, d), lambda i: (i, 0))
    in_specs = [row, _const_spec((1, d)), _layer_spec(wg, layer), _layer_spec(wu, layer), _layer_spec(wd, layer)]
    args = [x, g.reshape(1, d), wg, wu, wd]
    if final_g is not None:
        in_specs.append(_const_spec((1, d)))
        args.append(final_g.reshape(1, d))
    return pl.pallas_call(
        functools.partial(_ffn_kernel, final_norm=final_g is not None),
        grid=(n // tm,),
        in_specs=in_specs,
        out_specs=row,
        out_shape=jax.ShapeDtypeStruct((n, d), F32),
        compiler_params=pltpu.CompilerParams(dimension_semantics=("arbitrary",), vmem_limit_bytes=VMEM_LIMIT),
        name="ffn",
    )(*args)


def _inproj_kernel(x_ref, g_ref, w_ref, cos_ref, sa_ref, sb_ref, lq1_ref, lk1_ref, lq2_ref, lk2_ref, *rest,
                   lam_init, has_prev, v_transposed):
    q_ref, k_ref, kb_ref, v_ref, vb_ref, u_ref, cv_ref, bg_ref, lam_ref = rest[2:] if has_prev else rest
    tm = x_ref.shape[0]

    @pl.when(pl.program_id(0) == 0)
    def _():
        a = jnp.sum(lq1_ref[...] * lk1_ref[...], axis=-1, keepdims=True)
        b = jnp.sum(lq2_ref[...] * lk2_ref[...], axis=-1, keepdims=True)
        lam_ref[...] = jnp.broadcast_to(jnp.exp(a) - jnp.exp(b) + lam_init, lam_ref.shape)

    h = _rms(x_ref[...], g_ref[...]).astype(BF16)
    z = jnp.dot(h, w_ref[...].astype(BF16), preferred_element_type=F32)
    cos, sa, sb = cos_ref[...], sa_ref[...], sb_ref[...]

    def rope(t):
        return t * cos + pltpu.roll(t, 96, 1) * sa + pltpu.roll(t, 32, 1) * sb

    for hd in range(A_HEADS):
        sl = slice(hd * A_VDIM, (hd + 1) * A_VDIM)
        qh = rope(z[:, hd * A_VDIM:(hd + 1) * A_VDIM])
        q_ref[:, sl] = (qh * (ATTN_SCALE * LOG2E)).astype(BF16)
        kh = rope(z[:, A_WIDTH + hd * A_VDIM:A_WIDTH + (hd + 1) * A_VDIM])
        k_ref[pl.ds(hd, tm, stride=A_HEADS), :] = kh
        kb_ref[:, sl] = kh.astype(BF16)
    v = z[:, 2 * A_WIDTH:3 * A_WIDTH]
    for hd in range(A_HEADS):
        v_ref[pl.ds(hd, tm, stride=A_HEADS), :] = v[:, hd * A_VDIM:(hd + 1) * A_VDIM]
    if v_transposed:
        vt = v.T.astype(BF16)
        for hd in range(A_HEADS):
            vb_ref[hd, 0:A_VDIM, :] = vt[hd * A_VDIM:(hd + 1) * A_VDIM, :]
            vb_ref[hd, A_VDIM:VT_ROWS, :] = jnp.ones((BF16_ROWS, tm), BF16)
    else:
        vb_ref[...] = v.astype(BF16)
    o = 3 * A_WIDTH
    u_ref[...] = z[:, o:o + B_WIDTH]
    hc = z[:, o + B_WIDTH:o + B_WIDTH + C_WIDTH]
    bg_ref[...] = z[:, o + B_WIDTH + C_WIDTH:o + B_WIDTH + 2 * C_WIDTH]
    cg = z[:, o + B_WIDTH + 2 * C_WIDTH:o + B_WIDTH + 3 * C_WIDTH]
    cv_ref[...] = cg * hc


def _rope_tables(pos):
    half = A_HEAD_DIM // 2
    inv = np.power(ROPE_THETA, -np.arange(half, dtype=np.float64) / half)
    ang = pos.astype(np.float64)[:, None] * inv[None, :]
    cos, sin, zero = np.cos(ang), np.sin(ang), np.zeros_like(ang)
    cos_t = np.concatenate([cos, cos, cos, cos], axis=1)
    sa = np.concatenate([-sin, zero, -sin, zero], axis=1)
    sb = np.concatenate([zero, sin, zero, sin], axis=1)
    return [jnp.asarray(t, F32) for t in (cos_t, sa, sb)]


def _inproj(x, g, w_in, tables, lams, lam_init, layer, kv_prev, *, tm, seq, v_transposed):
    n, d = x.shape
    tps = seq // tm
    if v_transposed:
        vb_shape = jax.ShapeDtypeStruct((n // seq, A_HEADS, VT_ROWS, seq), BF16)
        vb_spec = pl.BlockSpec((None, A_HEADS, VT_ROWS, tm), lambda i: (i // tps, 0, 0, i % tps))
    else:
        vb_shape = jax.ShapeDtypeStruct((n, A_WIDTH), BF16)
        vb_spec = pl.BlockSpec((tm, A_WIDTH), lambda i: (i, 0))
    row = lambda w: pl.BlockSpec((tm, w), lambda i: (i, 0))
    tab = pl.BlockSpec((tm, A_VDIM), lambda i: (i % tps, 0))
    lam_spec = _const_spec((1, A_HEAD_DIM))
    leaf = jax.ShapeDtypeStruct((DEPTH, n * A_HEADS, A_VDIM), F32)
    leaf_spec = pl.BlockSpec((None, tm * A_HEADS, A_VDIM), lambda i: (layer, i, 0))
    out_shape = [
        jax.ShapeDtypeStruct((n, A_WIDTH), BF16),
        leaf,
        jax.ShapeDtypeStruct((n, A_WIDTH), BF16),
        leaf,
        vb_shape,
        jax.ShapeDtypeStruct((n, B_WIDTH), F32),
        jax.ShapeDtypeStruct((n, C_WIDTH), F32),
        jax.ShapeDtypeStruct((n, C_WIDTH), F32),
        jax.ShapeDtypeStruct((8, A_VDIM), F32),
    ]
    out_specs = [row(A_WIDTH), leaf_spec, row(A_WIDTH), leaf_spec, vb_spec, row(B_WIDTH), row(C_WIDTH),
                 row(C_WIDTH), pl.BlockSpec((8, A_VDIM), lambda i: (0, 0))]
    in_specs = [row(d), _const_spec((1, d)), _layer_spec(w_in, layer), tab, tab, tab] + [lam_spec] * 4
    args = [x, g.reshape(1, d), w_in, *tables, *[v.reshape(1, A_HEAD_DIM) for v in lams]]
    aliases = {}
    if kv_prev is not None:
        aliases = {len(args): 1, len(args) + 1: 3}
        in_specs += [pl.BlockSpec(memory_space=pl.ANY)] * 2
        args += list(kv_prev)
    return pl.pallas_call(
        functools.partial(_inproj_kernel, lam_init=lam_init, has_prev=kv_prev is not None,
                          v_transposed=v_transposed),
        grid=(n // tm,),
        in_specs=in_specs,
        out_specs=out_specs,
        out_shape=out_shape,
        input_output_aliases=aliases,
        compiler_params=pltpu.CompilerParams(dimension_semantics=("arbitrary",), vmem_limit_bytes=VMEM_LIMIT),
        name="inproj",
    )(*args)


def _stack_q(q):
    lane = lax.broadcasted_iota(jnp.int32, q.shape, 1)
    zero = jnp.zeros_like(q)
    return jnp.concatenate([jnp.where(lane < A_HEAD_DIM, q, zero), jnp.where(lane >= A_HEAD_DIM, q, zero)], axis=0)


def _qk(qs, k):
    return lax.dot_general(qs, k, (((1,), (1,)), ((), ())), preferred_element_type=F32)


def _attn_finish(acc, l, lam_ref, sg_ref, tq, lam_init):
    o = acc[:tq] / l[:tq] - lam_ref[0:1, :] * (acc[tq:] / l[tq:])
    return _rms(o, sg_ref[...]) * (1.0 - lam_init)


def _attn_prompt_kernel(q_ref, k_ref, vt_ref, lam_ref, sg_ref, o_ref, qs_ref, s_ref, m_ref, acc_ref, *, tq, lam_init):
    qi = pl.program_id(1)
    for hd in range(A_HEADS):
        qs_ref[hd] = _stack_q(q_ref[:, hd * A_VDIM:(hd + 1) * A_VDIM])
    m_ref[...] = jnp.full(m_ref.shape, NEG_INF, F32)
    acc_ref[...] = jnp.zeros(acc_ref.shape, F32)

    def scores_into(j, hd):
        start = pl.multiple_of(j * tq, tq)
        s_ref[hd] = _qk(k_ref[pl.ds(start, tq), hd * A_VDIM:(hd + 1) * A_VDIM], qs_ref[hd])

    def softmax_pv(j, hd, visible):
        start = pl.multiple_of(j * tq, tq)
        st = s_ref[hd]
        if visible is not None:
            st = jnp.where(visible, st, NEG_INF)
        m_old = m_ref[hd]
        m_new = jnp.maximum(m_old, jnp.max(st, axis=0, keepdims=True))
        alpha = jnp.exp2(m_old - m_new)
        pt = jnp.exp2(st - m_new)
        m_ref[hd] = m_new
        acc_ref[hd] = alpha * acc_ref[hd] + jnp.dot(vt_ref[hd, :, pl.ds(start, tq)], pt.astype(BF16),
                                                    preferred_element_type=F32)

    for hd in range(A_HEADS):
        scores_into(0, hd)

    def body(j, carry):
        for hd in range(A_HEADS):
            softmax_pv(j, hd, None)
            scores_into(j + 1, hd)
        return carry

    lax.fori_loop(0, qi, body, 0)
    kc = lax.broadcasted_iota(jnp.int32, (tq, 2 * tq), 0) // CHUNK
    qc = (lax.broadcasted_iota(jnp.int32, (tq, 2 * tq), 1) % tq) // CHUNK
    visible = kc <= qc
    for hd in range(A_HEADS):
        softmax_pv(qi, hd, visible)
    lam = lam_ref[0:1, 0:1]
    for hd in range(A_HEADS):
        inv = 1.0 / acc_ref[hd, A_VDIM:A_VDIM + 1, :]
        acc = acc_ref[hd, 0:A_VDIM, :]
        ot = acc[:, :tq] * inv[:, :tq] - (lam * inv[:, tq:]) * acc[:, tq:]
        ot = ot * lax.rsqrt(jnp.mean(ot * ot, axis=0, keepdims=True) + EPS) * (sg_ref[...] * (1.0 - lam_init))
        o_ref[:, hd * A_VDIM:(hd + 1) * A_VDIM] = ot.T.astype(o_ref.dtype)


def _attn_prompt(q, k, vt, lam, sg, lam_init, *, bsz, seq, tq):
    q, k = (t.reshape(bsz, seq, A_WIDTH) for t in (q, k))
    qspec = pl.BlockSpec((None, tq, A_WIDTH), lambda b, i: (b, i, 0))
    kspec = pl.BlockSpec((None, seq, A_WIDTH), lambda b, i: (b, 0, 0))
    vspec = pl.BlockSpec((None, A_HEADS, VT_ROWS, seq), lambda b, i: (b, 0, 0, 0))
    out = pl.pallas_call(
        functools.partial(_attn_prompt_kernel, tq=tq, lam_init=lam_init),
        grid=(bsz, seq // tq),
        in_specs=[qspec, kspec, vspec,
                  pl.BlockSpec((8, A_VDIM), lambda b, i: (0, 0)),
                  pl.BlockSpec((A_VDIM, 1), lambda b, i: (0, 0))],
        out_specs=qspec,
        out_shape=jax.ShapeDtypeStruct((bsz, seq, A_WIDTH), BF16),
        scratch_shapes=[pltpu.VMEM((A_HEADS, 2 * tq, A_VDIM), BF16),
                        pltpu.VMEM((A_HEADS, tq, 2 * tq), F32),
                        pltpu.VMEM((A_HEADS, 1, 2 * tq), F32),
                        pltpu.VMEM((A_HEADS, VT_ROWS, 2 * tq), F32)],
        compiler_params=pltpu.CompilerParams(dimension_semantics=("arbitrary",) * 2, vmem_limit_bytes=VMEM_LIMIT),
        name="attn_prompt",
    )(q, k, vt, lam, sg.reshape(A_VDIM, 1))
    return out.reshape(bsz * seq, A_WIDTH)


def _attn_sample_kernel(q_ref, kc_ref, vc_ref, kn_ref, vn_ref, lam_ref, sg_ref, o_ref, *, tq, past, lam_init):
    for hd in range(A_HEADS):
        hs = slice(hd * A_VDIM, (hd + 1) * A_VDIM)
        qs = _stack_q(q_ref[:, hs])
        sc = _qk(qs, kc_ref[pl.ds(hd, past, stride=A_HEADS), :].astype(BF16))
        sn = _qk(qs, kn_ref[:, hs])
        m = jnp.maximum(jnp.max(sc, axis=-1, keepdims=True), jnp.max(sn, axis=-1, keepdims=True))
        pc = jnp.exp2(sc - m)
        pn = jnp.exp2(sn - m)
        l = jnp.sum(pc, axis=-1, keepdims=True) + jnp.sum(pn, axis=-1, keepdims=True)
        vc = vc_ref[pl.ds(hd, past, stride=A_HEADS), :].astype(BF16)
        acc = (jnp.dot(pc.astype(BF16), vc, preferred_element_type=F32)
               + jnp.dot(pn.astype(BF16), vn_ref[:, hs], preferred_element_type=F32))
        o_ref[:, hs] = _attn_finish(acc, l, lam_ref, sg_ref, tq, lam_init).astype(o_ref.dtype)


def _attn_sample(q, k, v, cache_k, cache_v, layer, lam, sg, lam_init, *, bsz, seq):
    past = cache_k.shape[2]
    q, k, v = (t.reshape(bsz, seq, A_WIDTH) for t in (q, k, v))
    ck = cache_k.reshape(DEPTH, bsz, past * A_HEADS, A_VDIM)
    cv = cache_v.reshape(DEPTH, bsz, past * A_HEADS, A_VDIM)
    nspec = pl.BlockSpec((None, seq, A_WIDTH), lambda b: (b, 0, 0))
    cspec = pl.BlockSpec((None, None, past * A_HEADS, A_VDIM), lambda b: (layer, b, 0, 0))
    out = pl.pallas_call(
        functools.partial(_attn_sample_kernel, tq=seq, past=past, lam_init=lam_init),
        grid=(bsz,),
        in_specs=[nspec, cspec, cspec, nspec, nspec,
                  pl.BlockSpec((8, A_VDIM), lambda b: (0, 0)),
                  pl.BlockSpec((1, A_VDIM), lambda b: (0, 0))],
        out_specs=nspec,
        out_shape=jax.ShapeDtypeStruct((bsz, seq, A_WIDTH), BF16),
        compiler_params=pltpu.CompilerParams(dimension_semantics=("arbitrary",), vmem_limit_bytes=VMEM_LIMIT),
        name="attn_sample",
    )(q, ck, cv, k, v, lam, sg.reshape(1, A_VDIM))
    return out.reshape(bsz * seq, A_WIDTH)


def _outproj_kernel(x_ref, ya_ref, u_ref, uh_ref, cv_ref, cvh_ref, bg_ref, wp_ref, ps_ref, cw_ref, wo_ref,
                    *rest, tm, tps, pos_off, zero_first, n_ffn):
    ffn_refs, (o_ref, uext_ref, cext_ref) = rest[:n_ffn], rest[n_ffn:]
    it = pl.program_id(0) % tps
    y = x_ref[...] + jnp.dot(ya_ref[...], wo_ref[0:A_WIDTH, :], preferred_element_type=F32)
    uh, cvh = uh_ref[...], cvh_ref[...]
    if zero_first:
        keep = (it > 0).astype(F32)
        uh, cvh = uh * keep, cvh * keep
    cur = u_ref[...]
    uext_ref[0:HALO, :] = uh
    uext_ref[HALO:, :] = cur
    cv = cv_ref[...]
    cext_ref[0:HALO, :] = cvh
    cext_ref[HALO:, :] = cv

    run = cur
    sums = {}
    for j in range(1, max(POOL_WINDOWS)):
        run = run + uext_ref[pl.ds(HALO - j, tm), :]
        if j + 1 in POOL_WINDOWS:
            sums[j + 1] = run
    pos = pos_off + it * tm + lax.broadcasted_iota(jnp.int32, (tm, 1), 0)
    lane = lax.broadcasted_iota(jnp.int32, (tm, B_WIDTH), 1)
    d = None
    for gi, w in reversed(list(enumerate(POOL_WINDOWS))):
        mean = sums[w] / jnp.minimum(pos + 1, w).astype(F32)
        d = mean if d is None else jnp.where(lane < (gi + 1) * POOL_GROUP, mean, d)
    d = d - cur
    y_pool = jnp.dot(d.astype(BF16), wp_ref[...], preferred_element_type=F32) * ps_ref[...]

    cw = cw_ref[...]
    conv = cw[0:1, :] * cext_ref[pl.ds(HALO - 2, tm), :] + cw[1:2, :] * cext_ref[pl.ds(HALO - 1, tm), :]
    conv = conv + cw[2:3, :] * cv
    y_conv = bg_ref[...] * conv

    mixed = jnp.concatenate([y_pool.astype(BF16), y_conv.astype(BF16)], axis=-1)
    y = y + jnp.dot(mixed, wo_ref[A_WIDTH:, :], preferred_element_type=F32)
    if ffn_refs:
        y = _ffn_body(y, *ffn_refs[:4], ffn_refs[4] if n_ffn == 5 else None)
    o_ref[...] = y


def _outproj(x, ya, u, cv, bg, u_halo, cv_halo, halo_map, wp, ps, cw, wo, layer, ffn=None, *, tm, seq, pos_off,
             zero_first):
    n, d = x.shape
    tps = seq // tm
    row = lambda w: pl.BlockSpec((tm, w), lambda i: (i, 0))
    halo = pl.BlockSpec((HALO, B_WIDTH), halo_map)
    in_specs = [row(d), row(A_WIDTH), row(B_WIDTH), halo, row(C_WIDTH), halo, row(C_WIDTH),
                _const_spec(wp.shape), _const_spec((1, B_WIDTH)), _const_spec(cw.shape), _layer_spec(wo, layer)]
    args = [x, ya, u, u_halo, cv, cv_halo, bg, wp, ps.reshape(1, B_WIDTH), cw, wo]
    if ffn is not None:
        g, wg, wu, wd, final_g = ffn
        in_specs += [_const_spec((1, d)), _layer_spec(wg, layer), _layer_spec(wu, layer), _layer_spec(wd, layer)]
        args += [g.reshape(1, d), wg, wu, wd]
        if final_g is not None:
            in_specs.append(_const_spec((1, d)))
            args.append(final_g.reshape(1, d))
    return pl.pallas_call(
        functools.partial(_outproj_kernel, tm=tm, tps=tps, pos_off=pos_off, zero_first=zero_first,
                          n_ffn=len(args) - 11),
        grid=(n // tm,),
        in_specs=in_specs,
        out_specs=row(d),
        out_shape=jax.ShapeDtypeStruct((n, d), F32),
        scratch_shapes=[pltpu.VMEM((HALO + tm, B_WIDTH), F32), pltpu.VMEM((HALO + tm, C_WIDTH), F32)],
        compiler_params=pltpu.CompilerParams(dimension_semantics=("arbitrary",), vmem_limit_bytes=VMEM_LIMIT),
        name="outproj_ffn" if ffn is not None else "outproj",
    )(*args)


def _block_diag(w):
    g, c, _ = w.shape
    eye = jnp.eye(g, dtype=w.dtype)
    return (eye[:, None, :, None] * w[:, :, None, :]).reshape(g * c, g * c)


def _front_pad(state, rows):
    b, r, c = state.shape
    return jnp.pad(state, ((0, 0), (rows - r, 0), (0, 0))).reshape(b * rows, c)


def _trunk(x, pos0, cache, p, *, tm_ffn, tm_mix, tq):
    bsz, seq, d = x.shape
    n = bsz * seq
    x = x.reshape(n, d)
    tables = _rope_tables(pos0 + np.arange(seq))
    kv_leaves, new_pool, new_conv = None, [], []
    for l in range(DEPTH):
        lam_init = 0.8 - 0.6 * math.exp(-0.3 * l)
        x = _ffn(x, p['norm_ffn1'][l], p['ffn1_gate'], p['ffn1_up'], p['ffn1_down'], l, tm=tm_ffn)
        lams = [p[name][l] for name in ('lambda_q1', 'lambda_k1', 'lambda_q2', 'lambda_k2')]
        q, k_all, kb, v_all, vb, u, cv, bg, lam = _inproj(x, p['norm_mix'][l], p['w_in'], tables, lams, lam_init,
                                                          l, kv_leaves, tm=tm_mix, seq=seq,
                                                          v_transposed=cache is None)
        kv_leaves = (k_all, v_all)
        if cache is None:
            ya = _attn_prompt(q, kb, vb, lam, p['subln'][l], lam_init, bsz=bsz, seq=seq, tq=tq)
            u_halo, cv_halo = u, cv
            per = tm_mix // HALO
            halo_map = lambda i, per=per: (jnp.maximum(i * per - 1, 0), 0)
            pool_ext, conv_ext = u.reshape(bsz, seq, B_WIDTH), cv.reshape(bsz, seq, C_WIDTH)
        else:
            cache_k, cache_v, state_pool, state_conv = cache
            ya = _attn_sample(q, kb, vb, cache_k, cache_v, l, lam, p['subln'][l], lam_init, bsz=bsz, seq=seq)
            u_halo = _front_pad(state_pool[l], HALO)
            cv_halo = _front_pad(state_conv[l], HALO)
            halo_map = lambda i: (i, 0)
            pool_ext = jnp.concatenate([state_pool[l], u.reshape(bsz, seq, B_WIDTH)], axis=1)
            conv_ext = jnp.concatenate([state_conv[l], cv.reshape(bsz, seq, C_WIDTH)], axis=1)
        final_g = p['final_norm'] if l == DEPTH - 1 else None
        ffn2 = (p['norm_ffn2'][l], p['ffn2_gate'], p['ffn2_up'], p['ffn2_down'], final_g)
        fuse = tm_mix == tm_ffn
        x = _outproj(x, ya, u, cv, bg, u_halo, cv_halo, halo_map,
                     p['pool_w_bd'][l], p['pool_scale'][l], p['conv_w'][l], p['w_out'], l, ffn2 if fuse else None,
                     tm=tm_mix, seq=seq, pos_off=pos0, zero_first=cache is None)
        if not fuse:
            x = _ffn(x, *ffn2[:4], l, final_g, tm=tm_ffn)
        new_pool.append(pool_ext[:, -POOL_HIST:])
        new_conv.append(conv_ext[:, -(CONV_W - 1):])
    new_k, new_v = (t.reshape(DEPTH, bsz, seq, A_HEADS, A_VDIM) for t in kv_leaves)
    return (x.reshape(bsz, seq, d), new_k, new_v, jnp.stack(new_pool), jnp.stack(new_conv))


def kernel(x_prompt, x_sample, cache_k, cache_v, state_pool, state_conv, norm_ffn1, ffn1_gate, ffn1_up, ffn1_down,
           norm_mix, w_in, lambda_q1, lambda_k1, lambda_q2, lambda_k2, subln, pool_w, pool_scale, conv_w, w_out,
           norm_ffn2, ffn2_gate, ffn2_up, ffn2_down, final_norm):
    p = {
        'norm_ffn1': norm_ffn1, 'norm_mix': norm_mix, 'norm_ffn2': norm_ffn2, 'final_norm': final_norm,
        'lambda_q1': lambda_q1, 'lambda_k1': lambda_k1, 'lambda_q2': lambda_q2, 'lambda_k2': lambda_k2,
        'subln': subln, 'pool_scale': pool_scale, 'conv_w': conv_w,
        'ffn1_gate': ffn1_gate, 'ffn1_up': ffn1_up, 'ffn1_down': ffn1_down,
        'ffn2_gate': ffn2_gate, 'ffn2_up': ffn2_up, 'ffn2_down': ffn2_down,
        'w_in': w_in, 'w_out': w_out.astype(BF16),
        'pool_w_bd': jnp.stack([_block_diag(pool_w[l]) for l in range(DEPTH)]).astype(BF16),
    }
    y_p, k_p, v_p, pool_p, conv_p = _trunk(x_prompt, 0, None, p, tm_ffn=512, tm_mix=512, tq=256)
    y_s, k_s, v_s, pool_s, conv_s = _trunk(x_sample, PAST_LEN, (cache_k, cache_v, state_pool, state_conv), p,
                                           tm_ffn=256, tm_mix=32, tq=None)
    return (y_p, y_s, k_p, v_p, pool_p, conv_p, k_s, v_s, pool_s, conv_s)
```

```python
import functools
import math

import numpy as np
import jax
import jax.numpy as jnp
from jax import lax
from jax.experimental import pallas as pl
from jax.experimental.pallas import tpu as pltpu

F32 = jnp.float32
BF16 = jnp.bfloat16

D_MODEL = 1024
DEPTH = 2
PAST_LEN = 4096
CHUNK = 64
A_WIDTH = 512
A_HEAD_DIM = 64
A_HEADS = 4
A_VDIM = 128
ATTN_SCALE = A_HEAD_DIM ** -0.5
B_WIDTH = 256
POOL_WINDOWS = (2, 4, 8, 16)
POOL_GROUP = 64
POOL_HIST = 15
C_WIDTH = 256
CONV_W = 3
D_FF = 2816
ROPE_THETA = 10000.0
EPS = 1e-6
NEG_INF = -1e30
D_IN = 3 * A_WIDTH + B_WIDTH + 3 * C_WIDTH

LOG2E = math.log2(math.e)
BF16_ROWS = 16
VT_ROWS = A_VDIM + BF16_ROWS
HALO = 16
FF_CHUNK = 256
VMEM_LIMIT = 56 * 1024 * 1024


def _rms(x, g):
    return x * lax.rsqrt(jnp.mean(x * x, axis=-1, keepdims=True) + EPS) * g


def _const_spec(shape):
    return pl.BlockSpec(shape, lambda *_: (0,) * len(shape), pipeline_mode=pl.Buffered(1))


def _layer_spec(w, layer):
    tail = w.shape[1:]
    return pl.BlockSpec((None,) + tail, lambda *_: (layer,) + (0,) * len(tail), pipeline_mode=pl.Buffered(1))


def _ffn_kernel(x_ref, g_ref, wg_ref, wu_ref, wd_ref, *rest, final_norm):
    fg_ref = rest[0] if final_norm else None
    rest[-1][...] = _ffn_body(x_ref[...], g_ref, wg_ref, wu_ref, wd_ref, fg_ref, single_down=True)


def _ffn_body(x, g_ref, wg_ref, wu_ref, wd_ref, fg_ref, single_down=False):
    h = _rms(x, g_ref[...]).astype(BF16)
    acc = jnp.zeros(x.shape, F32)
    acts = []
    for c in range(D_FF // FF_CHUNK):
        sl = slice(c * FF_CHUNK, (c + 1) * FF_CHUNK)
        gate = jnp.dot(h, wg_ref[:, sl].astype(BF16), preferred_element_type=F32)
        up = jnp.dot(h, wu_ref[:, sl].astype(BF16), preferred_element_type=F32)
        act = (gate * (1.0 / (1.0 + jnp.exp(-gate))) * up).astype(BF16)
        if single_down:
            acts.append(act)
        else:
            acc = acc + jnp.dot(act, wd_ref[sl, :].astype(BF16), preferred_element_type=F32)
    if single_down:
        acc = jnp.dot(jnp.concatenate(acts, axis=-1), wd_ref[...].astype(BF16), preferred_element_type=F32)
    y = x + 0.5 * acc
    return y if fg_ref is None else _rms(y, fg_ref[...])


def _ffn(x, g, wg, wu, wd, layer, final_g=None, *, tm):
    n, d = x.shape
    assert n % tm == 0
    row = pl.BlockSpec((tm, d), lambda i: (i, 0))
    in_specs = [row, _const_spec((1, d)), _layer_spec(wg, layer), _layer_spec(wu, layer), _layer_spec(wd, layer)]
    args = [x, g.reshape(1, d), wg, wu, wd]
    if final_g is not None:
        in_specs.append(_const_spec((1, d)))
        args.append(final_g.reshape(1, d))
    return pl.pallas_call(
        functools.partial(_ffn_kernel, final_norm=final_g is not None),
        grid=(n // tm,),
        in_specs=in_specs,
        out_specs=row,
        out_shape=jax.ShapeDtypeStruct((n, d), F32),
        compiler_params=pltpu.CompilerParams(dimension_semantics=("arbitrary",), vmem_limit_bytes=VMEM_LIMIT),
        name="ffn",
    )(*args)


def _inproj_kernel(x_ref, g_ref, w_ref, cos_ref, sa_ref, sb_ref, lq1_ref, lk1_ref, lq2_ref, lk2_ref, *rest,
                   lam_init, has_prev, v_transposed):
    q_ref, k_ref, kb_ref, v_ref, vb_ref, u_ref, cv_ref, bg_ref, lam_ref = rest[2:] if has_prev else rest
    tm = x_ref.shape[0]

    @pl.when(pl.program_id(0) == 0)
    def _():
        a = jnp.sum(lq1_ref[...] * lk1_ref[...], axis=-1, keepdims=True)
        b = jnp.sum(lq2_ref[...] * lk2_ref[...], axis=-1, keepdims=True)
        lam_ref[...] = jnp.broadcast_to(jnp.exp(a) - jnp.exp(b) + lam_init, lam_ref.shape)

    h = _rms(x_ref[...], g_ref[...]).astype(BF16)
    z = jnp.dot(h, w_ref[...].astype(BF16), preferred_element_type=F32)
    cos, sa, sb = cos_ref[...], sa_ref[...], sb_ref[...]

    def rope(t):
        return t * cos + pltpu.roll(t, 96, 1) * sa + pltpu.roll(t, 32, 1) * sb

    for hd in range(A_HEADS):
        sl = slice(hd * A_VDIM, (hd + 1) * A_VDIM)
        qh = rope(z[:, hd * A_VDIM:(hd + 1) * A_VDIM])
        q_ref[:, sl] = (qh * (ATTN_SCALE * LOG2E)).astype(BF16)
        kh = rope(z[:, A_WIDTH + hd * A_VDIM:A_WIDTH + (hd + 1) * A_VDIM])
        k_ref[pl.ds(hd, tm, stride=A_HEADS), :] = kh
        kb_ref[:, sl] = kh.astype(BF16)
    v = z[:, 2 * A_WIDTH:3 * A_WIDTH]
    for hd in range(A_HEADS):
        v_ref[pl.ds(hd, tm, stride=A_HEADS), :] = v[:, hd * A_VDIM:(hd + 1) * A_VDIM]
    if v_transposed:
        vt = v.T.astype(BF16)
        for hd in range(A_HEADS):
            vb_ref[hd, 0:A_VDIM, :] = vt[hd * A_VDIM:(hd + 1) * A_VDIM, :]
            vb_ref[hd, A_VDIM:VT_ROWS, :] = jnp.ones((BF16_ROWS, tm), BF16)
    else:
        vb_ref[...] = v.astype(BF16)
    o = 3 * A_WIDTH
    u_ref[...] = z[:, o:o + B_WIDTH]
    hc = z[:, o + B_WIDTH:o + B_WIDTH + C_WIDTH]
    bg_ref[...] = z[:, o + B_WIDTH + C_WIDTH:o + B_WIDTH + 2 * C_WIDTH]
    cg = z[:, o + B_WIDTH + 2 * C_WIDTH:o + B_WIDTH + 3 * C_WIDTH]
    cv_ref[...] = cg * hc


def _rope_tables(pos):
    half = A_HEAD_DIM // 2
    inv = np.power(ROPE_THETA, -np.arange(half, dtype=np.float64) / half)
    ang = pos.astype(np.float64)[:, None] * inv[None, :]
    cos, sin, zero = np.cos(ang), np.sin(ang), np.zeros_like(ang)
    cos_t = np.concatenate([cos, cos, cos, cos], axis=1)
    sa = np.concatenate([-sin, zero, -sin, zero], axis=1)
    sb = np.concatenate([zero, sin, zero, sin], axis=1)
    return [jnp.asarray(t, F32) for t in (cos_t, sa, sb)]


def _inproj(x, g, w_in, tables, lams, lam_init, layer, kv_prev, *, tm, seq, v_transposed):
    n, d = x.shape
    tps = seq // tm
    if v_transposed:
        vb_shape = jax.ShapeDtypeStruct((n // seq, A_HEADS, VT_ROWS, seq), BF16)
        vb_spec = pl.BlockSpec((None, A_HEADS, VT_ROWS, tm), lambda i: (i // tps, 0, 0, i % tps))
    else:
        vb_shape = jax.ShapeDtypeStruct((n, A_WIDTH), BF16)
        vb_spec = pl.BlockSpec((tm, A_WIDTH), lambda i: (i, 0))
    row = lambda w: pl.BlockSpec((tm, w), lambda i: (i, 0))
    tab = pl.BlockSpec((tm, A_VDIM), lambda i: (i % tps, 0))
    lam_spec = _const_spec((1, A_HEAD_DIM))
    leaf = jax.ShapeDtypeStruct((DEPTH, n * A_HEADS, A_VDIM), F32)
    leaf_spec = pl.BlockSpec((None, tm * A_HEADS, A_VDIM), lambda i: (layer, i, 0))
    out_shape = [
        jax.ShapeDtypeStruct((n, A_WIDTH), BF16),
        leaf,
        jax.ShapeDtypeStruct((n, A_WIDTH), BF16),
        leaf,
        vb_shape,
        jax.ShapeDtypeStruct((n, B_WIDTH), F32),
        jax.ShapeDtypeStruct((n, C_WIDTH), F32),
        jax.ShapeDtypeStruct((n, C_WIDTH), F32),
        jax.ShapeDtypeStruct((8, A_VDIM), F32),
    ]
    out_specs = [row(A_WIDTH), leaf_spec, row(A_WIDTH), leaf_spec, vb_spec, row(B_WIDTH), row(C_WIDTH),
                 row(C_WIDTH), pl.BlockSpec((8, A_VDIM), lambda i: (0, 0))]
    in_specs = [row(d), _const_spec((1, d)), _layer_spec(w_in, layer), tab, tab, tab] + [lam_spec] * 4
    args = [x, g.reshape(1, d), w_in, *tables, *[v.reshape(1, A_HEAD_DIM) for v in lams]]
    aliases = {}
    if kv_prev is not None:
        aliases = {len(args): 1, len(args) + 1: 3}
        in_specs += [pl.BlockSpec(memory_space=pl.ANY)] * 2
        args += list(kv_prev)
    return pl.pallas_call(
        functools.partial(_inproj_kernel, lam_init=lam_init, has_prev=kv_prev is not None,
                          v_transposed=v_transposed),
        grid=(n // tm,),
        in_specs=in_specs,
        out_specs=out_specs,
        out_shape=out_shape,
        input_output_aliases=aliases,
        compiler_params=pltpu.CompilerParams(dimension_semantics=("arbitrary",), vmem_limit_bytes=VMEM_LIMIT),
        name="inproj",
    )(*args)


def _stack_q(q):
    lane = lax.broadcasted_iota(jnp.int32, q.shape, 1)
    zero = jnp.zeros_like(q)
    return jnp.concatenate([jnp.where(lane < A_HEAD_DIM, q, zero), jnp.where(lane >= A_HEAD_DIM, q, zero)], axis=0)


def _qk(qs, k):
    return lax.dot_general(qs, k, (((1,), (1,)), ((), ())), preferred_element_type=F32)


def _attn_finish(acc, l, lam_ref, sg_ref, tq, lam_init):
    o = acc[:tq] / l[:tq] - lam_ref[0:1, :] * (acc[tq:] / l[tq:])
    return _rms(o, sg_ref[...]) * (1.0 - lam_init)


def _attn_prompt_kernel(q_ref, k_ref, vt_ref, lam_ref, sg_ref, o_ref, qs_ref, s_ref, m_ref, acc_ref, *, tq, lam_init):
    qi = pl.program_id(1)
    for hd in range(A_HEADS):
        qs_ref[hd] = _stack_q(q_ref[:, hd * A_VDIM:(hd + 1) * A_VDIM])
    m_ref[...] = jnp.full(m_ref.shape, NEG_INF, F32)
    acc_ref[...] = jnp.zeros(acc_ref.shape, F32)

    def scores_into(j, hd):
        start = pl.multiple_of(j * tq, tq)
        s_ref[hd] = _qk(k_ref[pl.ds(start, tq), hd * A_VDIM:(hd + 1) * A_VDIM], qs_ref[hd])

    def softmax_pv(j, hd, visible):
        start = pl.multiple_of(j * tq, tq)
        st = s_ref[hd]
        if visible is not None:
            st = jnp.where(visible, st, NEG_INF)
        m_old = m_ref[hd]
        m_new = jnp.maximum(m_old, jnp.max(st, axis=0, keepdims=True))
        alpha = jnp.exp2(m_old - m_new)
        pt = jnp.exp2(st - m_new)
        m_ref[hd] = m_new
        acc_ref[hd] = alpha * acc_ref[hd] + jnp.dot(vt_ref[hd, :, pl.ds(start, tq)], pt.astype(BF16),
                                                    preferred_element_type=F32)

    for hd in range(A_HEADS):
        scores_into(0, hd)

    def body(j, carry):
        for hd in range(A_HEADS):
            softmax_pv(j, hd, None)
            scores_into(j + 1, hd)
        return carry

    lax.fori_loop(0, qi, body, 0)
    kc = lax.broadcasted_iota(jnp.int32, (tq, 2 * tq), 0) // CHUNK
    qc = (lax.broadcasted_iota(jnp.int32, (tq, 2 * tq), 1) % tq) // CHUNK
    visible = kc <= qc
    for hd in range(A_HEADS):
        softmax_pv(qi, hd, visible)
    lam = lam_ref[0:1, 0:1]
    for hd in range(A_HEADS):
        inv = 1.0 / acc_ref[hd, A_VDIM:A_VDIM + 1, :]
        acc = acc_ref[hd, 0:A_VDIM, :]
        ot = acc[:, :tq] * inv[:, :tq] - (lam * inv[:, tq:]) * acc[:, tq:]
        ot = ot * lax.rsqrt(jnp.mean(ot * ot, axis=0, keepdims=True) + EPS) * (sg_ref[...] * (1.0 - lam_init))
        o_ref[:, hd * A_VDIM:(hd + 1) * A_VDIM] = ot.T.astype(o_ref.dtype)


def _attn_prompt(q, k, vt, lam, sg, lam_init, *, bsz, seq, tq):
    q, k = (t.reshape(bsz, seq, A_WIDTH) for t in (q, k))
    qspec = pl.BlockSpec((None, tq, A_WIDTH), lambda b, i: (b, i, 0))
    kspec = pl.BlockSpec((None, seq, A_WIDTH), lambda b, i: (b, 0, 0))
    vspec = pl.BlockSpec((None, A_HEADS, VT_ROWS, seq), lambda b, i: (b, 0, 0, 0))
    out = pl.pallas_call(
        functools.partial(_attn_prompt_kernel, tq=tq, lam_init=lam_init),
        grid=(bsz, seq // tq),
        in_specs=[qspec, kspec, vspec,
                  pl.BlockSpec((8, A_VDIM), lambda b, i: (0, 0)),
                  pl.BlockSpec((A_VDIM, 1), lambda b, i: (0, 0))],
        out_specs=qspec,
        out_shape=jax.ShapeDtypeStruct((bsz, seq, A_WIDTH), BF16),
        scratch_shapes=[pltpu.VMEM((A_HEADS, 2 * tq, A_VDIM), BF16),
                        pltpu.VMEM((A_HEADS, tq, 2 * tq), F32),
                        pltpu.VMEM((A_HEADS, 1, 2 * tq), F32),
                        pltpu.VMEM((A_HEADS, VT_ROWS, 2 * tq), F32)],
        compiler_params=pltpu.CompilerParams(dimension_semantics=("arbitrary",) * 2, vmem_limit_bytes=VMEM_LIMIT),
        name="attn_prompt",
    )(q, k, vt, lam, sg.reshape(A_VDIM, 1))
    return out.reshape(bsz * seq, A_WIDTH)


def _attn_sample_kernel(q_ref, kc_ref, vc_ref, kn_ref, vn_ref, lam_ref, sg_ref, o_ref, *, tq, past, lam_init):
    for hd in range(A_HEADS):
        hs = slice(hd * A_VDIM, (hd + 1) * A_VDIM)
        qs = _stack_q(q_ref[:, hs])
        sc = _qk(qs, kc_ref[pl.ds(hd, past, stride=A_HEADS), :].astype(BF16))
        sn = _qk(qs, kn_ref[:, hs])
        m = jnp.maximum(jnp.max(sc, axis=-1, keepdims=True), jnp.max(sn, axis=-1, keepdims=True))
        pc = jnp.exp2(sc - m)
        pn = jnp.exp2(sn - m)
        l = jnp.sum(pc, axis=-1, keepdims=True) + jnp.sum(pn, axis=-1, keepdims=True)
        vc = vc_ref[pl.ds(hd, past, stride=A_HEADS), :].astype(BF16)
        acc = (jnp.dot(pc.astype(BF16), vc, preferred_element_type=F32)
               + jnp.dot(pn.astype(BF16), vn_ref[:, hs], preferred_element_type=F32))
        o_ref[:, hs] = _attn_finish(acc, l, lam_ref, sg_ref, tq, lam_init).astype(o_ref.dtype)


def _attn_sample(q, k, v, cache_k, cache_v, layer, lam, sg, lam_init, *, bsz, seq):
    past = cache_k.shape[2]
    q, k, v = (t.reshape(bsz, seq, A_WIDTH) for t in (q, k, v))
    ck = cache_k.reshape(DEPTH, bsz, past * A_HEADS, A_VDIM)
    cv = cache_v.reshape(DEPTH, bsz, past * A_HEADS, A_VDIM)
    nspec = pl.BlockSpec((None, seq, A_WIDTH), lambda b: (b, 0, 0))
    cspec = pl.BlockSpec((None, None, past * A_HEADS, A_VDIM), lambda b: (layer, b, 0, 0))
    out = pl.pallas_call(
        functools.partial(_attn_sample_kernel, tq=seq, past=past, lam_init=lam_init),
        grid=(bsz,),
        in_specs=[nspec, cspec, cspec, nspec, nspec,
                  pl.BlockSpec((8, A_VDIM), lambda b: (0, 0)),
                  pl.BlockSpec((1, A_VDIM), lambda b: (0, 0))],
        out_specs=nspec,
        out_shape=jax.ShapeDtypeStruct((bsz, seq, A_WIDTH), BF16),
        compiler_params=pltpu.CompilerParams(dimension_semantics=("arbitrary",), vmem_limit_bytes=VMEM_LIMIT),
        name="attn_sample",
    )(q, ck, cv, k, v, lam, sg.reshape(1, A_VDIM))
    return out.reshape(bsz * seq, A_WIDTH)


def _outproj_kernel(x_ref, ya_ref, u_ref, uh_ref, cv_ref, cvh_ref, bg_ref, wp_ref, ps_ref, cw_ref, wo_ref,
                    *rest, tm, tps, pos_off, zero_first, n_ffn):
    ffn_refs, (o_ref, uext_ref, cext_ref) = rest[:n_ffn], rest[n_ffn:]
    it = pl.program_id(0) % tps
    y = x_ref[...] + jnp.dot(ya_ref[...], wo_ref[0:A_WIDTH, :], preferred_element_type=F32)
    uh, cvh = uh_ref[...], cvh_ref[...]
    if zero_first:
        keep = (it > 0).astype(F32)
        uh, cvh = uh * keep, cvh * keep
    cur = u_ref[...]
    uext_ref[0:HALO, :] = uh
    uext_ref[HALO:, :] = cur
    cv = cv_ref[...]
    cext_ref[0:HALO, :] = cvh
    cext_ref[HALO:, :] = cv

    run = cur
    sums = {}
    for j in range(1, max(POOL_WINDOWS)):
        run = run + uext_ref[pl.ds(HALO - j, tm), :]
        if j + 1 in POOL_WINDOWS:
            sums[j + 1] = run
    pos = pos_off + it * tm + lax.broadcasted_iota(jnp.int32, (tm, 1), 0)
    lane = lax.broadcasted_iota(jnp.int32, (tm, B_WIDTH), 1)
    d = None
    for gi, w in reversed(list(enumerate(POOL_WINDOWS))):
        mean = sums[w] / jnp.minimum(pos + 1, w).astype(F32)
        d = mean if d is None else jnp.where(lane < (gi + 1) * POOL_GROUP, mean, d)
    d = d - cur
    y_pool = jnp.dot(d.astype(BF16), wp_ref[...], preferred_element_type=F32) * ps_ref[...]

    cw = cw_ref[...]
    conv = cw[0:1, :] * cext_ref[pl.ds(HALO - 2, tm), :] + cw[1:2, :] * cext_ref[pl.ds(HALO - 1, tm), :]
    conv = conv + cw[2:3, :] * cv
    y_conv = bg_ref[...] * conv

    mixed = jnp.concatenate([y_pool.astype(BF16), y_conv.astype(BF16)], axis=-1)
    y = y + jnp.dot(mixed, wo_ref[A_WIDTH:, :], preferred_element_type=F32)
    if ffn_refs:
        y = _ffn_body(y, *ffn_refs[:4], ffn_refs[4] if n_ffn == 5 else None, single_down=True)
    o_ref[...] = y


def _outproj(x, ya, u, cv, bg, u_halo, cv_halo, halo_map, wp, ps, cw, wo, layer, ffn=None, *, tm, seq, pos_off,
             zero_first):
    n, d = x.shape
    tps = seq // tm
    row = lambda w: pl.BlockSpec((tm, w), lambda i: (i, 0))
    halo = pl.BlockSpec((HALO, B_WIDTH), halo_map)
    in_specs = [row(d), row(A_WIDTH), row(B_WIDTH), halo, row(C_WIDTH), halo, row(C_WIDTH),
                _const_spec(wp.shape), _const_spec((1, B_WIDTH)), _const_spec(cw.shape), _layer_spec(wo, layer)]
    args = [x, ya, u, u_halo, cv, cv_halo, bg, wp, ps.reshape(1, B_WIDTH), cw, wo]
    if ffn is not None:
        g, wg, wu, wd, final_g = ffn
        in_specs += [_const_spec((1, d)), _layer_spec(wg, layer), _layer_spec(wu, layer), _layer_spec(wd, layer)]
        args += [g.reshape(1, d), wg, wu, wd]
        if final_g is not None:
            in_specs.append(_const_spec((1, d)))
            args.append(final_g.reshape(1, d))
    return pl.pallas_call(
        functools.partial(_outproj_kernel, tm=tm, tps=tps, pos_off=pos_off, zero_first=zero_first,
                          n_ffn=len(args) - 11),
        grid=(n // tm,),
        in_specs=in_specs,
        out_specs=row(d),
        out_shape=jax.ShapeDtypeStruct((n, d), F32),
        scratch_shapes=[pltpu.VMEM((HALO + tm, B_WIDTH), F32), pltpu.VMEM((HALO + tm, C_WIDTH), F32)],
        compiler_params=pltpu.CompilerParams(dimension_semantics=("arbitrary",), vmem_limit_bytes=VMEM_LIMIT),
        name="outproj_ffn" if ffn is not None else "outproj",
    )(*args)


def _block_diag(w):
    g, c, _ = w.shape
    eye = jnp.eye(g, dtype=w.dtype)
    return (eye[:, None, :, None] * w[:, :, None, :]).reshape(g * c, g * c)


def _front_pad(state, rows):
    b, r, c = state.shape
    return jnp.pad(state, ((0, 0), (rows - r, 0), (0, 0))).reshape(b * rows, c)


def _trunk(x, pos0, cache, p, *, tm_ffn, tm_mix, tq):
    bsz, seq, d = x.shape
    n = bsz * seq
    x = x.reshape(n, d)
    tables = _rope_tables(pos0 + np.arange(seq))
    kv_leaves, new_pool, new_conv = None, [], []
    for l in range(DEPTH):
        lam_init = 0.8 - 0.6 * math.exp(-0.3 * l)
        x = _ffn(x, p['norm_ffn1'][l], p['ffn1_gate'], p['ffn1_up'], p['ffn1_down'], l, tm=tm_ffn)
        lams = [p[name][l] for name in ('lambda_q1', 'lambda_k1', 'lambda_q2', 'lambda_k2')]
        q, k_all, kb, v_all, vb, u, cv, bg, lam = _inproj(x, p['norm_mix'][l], p['w_in'], tables, lams, lam_init,
                                                          l, kv_leaves, tm=tm_mix, seq=seq,
                                                          v_transposed=cache is None)
        kv_leaves = (k_all, v_all)
        if cache is None:
            ya = _attn_prompt(q, kb, vb, lam, p['subln'][l], lam_init, bsz=bsz, seq=seq, tq=tq)
            u_halo, cv_halo = u, cv
            per = tm_mix // HALO
            halo_map = lambda i, per=per: (jnp.maximum(i * per - 1, 0), 0)
            pool_ext, conv_ext = u.reshape(bsz, seq, B_WIDTH), cv.reshape(bsz, seq, C_WIDTH)
        else:
            cache_k, cache_v, state_pool, state_conv = cache
            ya = _attn_sample(q, kb, vb, cache_k, cache_v, l, lam, p['subln'][l], lam_init, bsz=bsz, seq=seq)
            u_halo = _front_pad(state_pool[l], HALO)
            cv_halo = _front_pad(state_conv[l], HALO)
            halo_map = lambda i: (i, 0)
            pool_ext = jnp.concatenate([state_pool[l], u.reshape(bsz, seq, B_WIDTH)], axis=1)
            conv_ext = jnp.concatenate([state_conv[l], cv.reshape(bsz, seq, C_WIDTH)], axis=1)
        final_g = p['final_norm'] if l == DEPTH - 1 else None
        ffn2 = (p['norm_ffn2'][l], p['ffn2_gate'], p['ffn2_up'], p['ffn2_down'], final_g)
        fuse = tm_mix == tm_ffn
        x = _outproj(x, ya, u, cv, bg, u_halo, cv_halo, halo_map,
                     p['pool_w_bd'][l], p['pool_scale'][l], p['conv_w'][l], p['w_out'], l, ffn2 if fuse else None,
                     tm=tm_mix, seq=seq, pos_off=pos0, zero_first=cache is None)
        if not fuse:
            x = _ffn(x, *ffn2[:4], l, final_g, tm=tm_ffn)
        new_pool.append(pool_ext[:, -POOL_HIST:])
        new_conv.append(conv_ext[:, -(CONV_W - 1):])
    new_k, new_v = (t.reshape(DEPTH, bsz, seq, A_HEADS, A_VDIM) for t in kv_leaves)
    return (x.reshape(bsz, seq, d), new_k, new_v, jnp.stack(new_pool), jnp.stack(new_conv))


def kernel(x_prompt, x_sample, cache_k, cache_v, state_pool, state_conv, norm_ffn1, ffn1_gate, ffn1_up, ffn1_down,
           norm_mix, w_in, lambda_q1, lambda_k1, lambda_q2, lambda_k2, subln, pool_w, pool_scale, conv_w, w_out,
           norm_ffn2, ffn2_gate, ffn2_up, ffn2_down, final_norm):
    p = {
        'norm_ffn1': norm_ffn1, 'norm_mix': norm_mix, 'norm_ffn2': norm_ffn2, 'final_norm': final_norm,
        'lambda_q1': lambda_q1, 'lambda_k1': lambda_k1, 'lambda_q2': lambda_q2, 'lambda_k2': lambda_k2,
        'subln': subln, 'pool_scale': pool_scale, 'conv_w': conv_w,
        'ffn1_gate': ffn1_gate, 'ffn1_up': ffn1_up, 'ffn1_down': ffn1_down,
        'ffn2_gate': ffn2_gate, 'ffn2_up': ffn2_up, 'ffn2_down': ffn2_down,
        'w_in': w_in, 'w_out': w_out.astype(BF16),
        'pool_w_bd': jnp.stack([_block_diag(pool_w[l]) for l in range(DEPTH)]).astype(BF16),
    }
    y_p, k_p, v_p, pool_p, conv_p = _trunk(x_prompt, 0, None, p, tm_ffn=512, tm_mix=512, tq=256)
    y_s, k_s, v_s, pool_s, conv_s = _trunk(x_sample, PAST_LEN, (cache_k, cache_v, state_pool, state_conv), p,
                                           tm_ffn=256, tm_mix=32, tq=None)
    return (y_p, y_s, k_p, v_p, pool_p, conv_p, k_s, v_s, pool_s, conv_s)
```

```python
import functools
import math

import numpy as np
import jax
import jax.numpy as jnp
from jax import lax
from jax.experimental import pallas as pl
from jax.experimental.pallas import tpu as pltpu

F32 = jnp.float32
BF16 = jnp.bfloat16

D_MODEL = 1024
DEPTH = 2
PAST_LEN = 4096
CHUNK = 64
A_WIDTH = 512
A_HEAD_DIM = 64
A_HEADS = 4
A_VDIM = 128
ATTN_SCALE = A_HEAD_DIM ** -0.5
B_WIDTH = 256
POOL_WINDOWS = (2, 4, 8, 16)
POOL_GROUP = 64
POOL_HIST = 15
C_WIDTH = 256
CONV_W = 3
D_FF = 2816
ROPE_THETA = 10000.0
EPS = 1e-6
NEG_INF = -1e30
D_IN = 3 * A_WIDTH + B_WIDTH + 3 * C_WIDTH

LOG2E = math.log2(math.e)
BF16_ROWS = 16
VT_ROWS = A_VDIM + BF16_ROWS
HALO = 16
FF_CHUNK = 256
VMEM_LIMIT = 56 * 1024 * 1024
PROMPT_TILES = dict(tm_ffn=512, tm_mix=512, tq=256)
SAMPLE_TILES = dict(tm_ffn=256, tm_mix=32, tq=None)


def _rms(x, g):
    return x * lax.rsqrt(jnp.mean(x * x, axis=-1, keepdims=True) + EPS) * g


def _const_spec(shape):
    return pl.BlockSpec(shape, lambda *_: (0,) * len(shape), pipeline_mode=pl.Buffered(1))


def _layer_spec(w, layer):
    tail = w.shape[1:]
    return pl.BlockSpec((None,) + tail, lambda *_: (layer,) + (0,) * len(tail), pipeline_mode=pl.Buffered(1))


def _ffn_kernel(x_ref, g_ref, wg_ref, wu_ref, wd_ref, *rest, final_norm):
    fg_ref = rest[0] if final_norm else None
    rest[-1][...] = _ffn_body(x_ref[...], g_ref, wg_ref, wu_ref, wd_ref, fg_ref, single_down=True)


def _ffn_body(x, g_ref, wg_ref, wu_ref, wd_ref, fg_ref, single_down=False):
    h = _rms(x, g_ref[...]).astype(BF16)
    acc = jnp.zeros(x.shape, F32)
    acts = []
    for c in range(D_FF // FF_CHUNK):
        sl = slice(c * FF_CHUNK, (c + 1) * FF_CHUNK)
        gate = jnp.dot(h, wg_ref[:, sl].astype(BF16), preferred_element_type=F32)
        up = jnp.dot(h, wu_ref[:, sl].astype(BF16), preferred_element_type=F32)
        act = (gate * (1.0 / (1.0 + jnp.exp(-gate))) * up).astype(BF16)
        if single_down:
            acts.append(act)
        else:
            acc = acc + jnp.dot(act, wd_ref[sl, :].astype(BF16), preferred_element_type=F32)
    if single_down:
        acc = jnp.dot(jnp.concatenate(acts, axis=-1), wd_ref[...].astype(BF16), preferred_element_type=F32)
    y = x + 0.5 * acc
    return y if fg_ref is None else _rms(y, fg_ref[...])


def _ffn(x, g, wg, wu, wd, layer, final_g=None, *, tm):
    n, d = x.shape
    assert n % tm == 0
    row = pl.BlockSpec((tm, d), lambda i: (i, 0))
    in_specs = [row, _const_spec((1, d)), _layer_spec(wg, layer), _layer_spec(wu, layer), _layer_spec(wd, layer)]
    args = [x, g.reshape(1, d), wg, wu, wd]
    if final_g is not None:
        in_specs.append(_const_spec((1, d)))
        args.append(final_g.reshape(1, d))
    return pl.pallas_call(
        functools.partial(_ffn_kernel, final_norm=final_g is not None),
        grid=(n // tm,),
        in_specs=in_specs,
        out_specs=row,
        out_shape=jax.ShapeDtypeStruct((n, d), F32),
        compiler_params=pltpu.CompilerParams(dimension_semantics=("arbitrary",), vmem_limit_bytes=VMEM_LIMIT),
        name="ffn",
    )(*args)


def _inproj_kernel(x_ref, g_ref, w_ref, cos_ref, sa_ref, sb_ref, lq1_ref, lk1_ref, lq2_ref, lk2_ref, *rest,
                   lam_init, has_prev, v_transposed):
    q_ref, k_ref, kb_ref, v_ref, vb_ref, u_ref, cv_ref, bg_ref, lam_ref = rest[2:] if has_prev else rest
    tm = x_ref.shape[0]

    @pl.when(pl.program_id(0) == 0)
    def _():
        a = jnp.sum(lq1_ref[...] * lk1_ref[...], axis=-1, keepdims=True)
        b = jnp.sum(lq2_ref[...] * lk2_ref[...], axis=-1, keepdims=True)
        lam_ref[...] = jnp.broadcast_to(jnp.exp(a) - jnp.exp(b) + lam_init, lam_ref.shape)

    h = _rms(x_ref[...], g_ref[...]).astype(BF16)
    z = jnp.dot(h, w_ref[...].astype(BF16), preferred_element_type=F32)
    z_qk, z = z[:, 0:2 * A_WIDTH], z[:, 2 * A_WIDTH:]
    cos, sa, sb = cos_ref[...], sa_ref[...], sb_ref[...]
    half = A_HEAD_DIM // 2

    def rope(t):
        return t * cos + pltpu.roll(t, A_VDIM - half, 1) * sa + pltpu.roll(t, half, 1) * sb

    for hd in range(A_HEADS):
        sl = slice(hd * A_VDIM, (hd + 1) * A_VDIM)
        qh = rope(z_qk[:, hd * A_VDIM:(hd + 1) * A_VDIM])
        q_ref[:, sl] = (qh * (ATTN_SCALE * LOG2E)).astype(BF16)
        kh = rope(z_qk[:, A_WIDTH + hd * A_VDIM:A_WIDTH + (hd + 1) * A_VDIM])
        k_ref[pl.ds(hd, tm, stride=A_HEADS), :] = kh
        kb_ref[:, sl] = kh.astype(BF16)
    v = z[:, 0:A_WIDTH]
    for hd in range(A_HEADS):
        v_ref[pl.ds(hd, tm, stride=A_HEADS), :] = v[:, hd * A_VDIM:(hd + 1) * A_VDIM]
    if v_transposed:
        vt = v.T.astype(BF16)
        for hd in range(A_HEADS):
            vb_ref[hd, 0:A_VDIM, :] = vt[hd * A_VDIM:(hd + 1) * A_VDIM, :]
            vb_ref[hd, A_VDIM:VT_ROWS, :] = jnp.ones((BF16_ROWS, tm), BF16)
    else:
        vb_ref[...] = v.astype(BF16)
    o = A_WIDTH
    u_ref[...] = z[:, o:o + B_WIDTH]
    hc = z[:, o + B_WIDTH:o + B_WIDTH + C_WIDTH]
    bg_ref[...] = z[:, o + B_WIDTH + C_WIDTH:o + B_WIDTH + 2 * C_WIDTH]
    cg = z[:, o + B_WIDTH + 2 * C_WIDTH:o + B_WIDTH + 3 * C_WIDTH]
    cv_ref[...] = cg * hc


def _rope_tables(pos):
    half = A_HEAD_DIM // 2
    inv = np.power(ROPE_THETA, -np.arange(half, dtype=np.float64) / half)
    ang = pos.astype(np.float64)[:, None] * inv[None, :]
    cos, sin, zero = np.cos(ang), np.sin(ang), np.zeros_like(ang)
    cos_t = np.concatenate([cos, cos, cos, cos], axis=1)
    sa = np.concatenate([-sin, zero, -sin, zero], axis=1)
    sb = np.concatenate([zero, sin, zero, sin], axis=1)
    return [jnp.asarray(t, F32) for t in (cos_t, sa, sb)]


def _inproj(x, g, w_in, tables, lams, lam_init, layer, kv_prev, *, tm, seq, v_transposed):
    n, d = x.shape
    tps = seq // tm
    if v_transposed:
        vb_shape = jax.ShapeDtypeStruct((n // seq, A_HEADS, VT_ROWS, seq), BF16)
        vb_spec = pl.BlockSpec((None, A_HEADS, VT_ROWS, tm), lambda i: (i // tps, 0, 0, i % tps))
    else:
        vb_shape = jax.ShapeDtypeStruct((n, A_WIDTH), BF16)
        vb_spec = pl.BlockSpec((tm, A_WIDTH), lambda i: (i, 0))
    row = lambda w: pl.BlockSpec((tm, w), lambda i: (i, 0))
    tab = pl.BlockSpec((tm, A_VDIM), lambda i: (i % tps, 0))
    lam_spec = _const_spec((1, A_HEAD_DIM))
    leaf = jax.ShapeDtypeStruct((DEPTH, n * A_HEADS, A_VDIM), F32)
    leaf_spec = pl.BlockSpec((None, tm * A_HEADS, A_VDIM), lambda i: (layer, i, 0))
    out_shape = [
        jax.ShapeDtypeStruct((n, A_WIDTH), BF16),
        leaf,
        jax.ShapeDtypeStruct((n, A_WIDTH), BF16),
        leaf,
        vb_shape,
        jax.ShapeDtypeStruct((n, B_WIDTH), F32),
        jax.ShapeDtypeStruct((n, C_WIDTH), F32),
        jax.ShapeDtypeStruct((n, C_WIDTH), F32),
        jax.ShapeDtypeStruct((8, A_VDIM), F32),
    ]
    out_specs = [row(A_WIDTH), leaf_spec, row(A_WIDTH), leaf_spec, vb_spec, row(B_WIDTH), row(C_WIDTH),
                 row(C_WIDTH), pl.BlockSpec((8, A_VDIM), lambda i: (0, 0))]
    in_specs = [row(d), _const_spec((1, d)), _layer_spec(w_in, layer), tab, tab, tab] + [lam_spec] * 4
    args = [x, g.reshape(1, d), w_in, *tables, *[v.reshape(1, A_HEAD_DIM) for v in lams]]
    aliases = {}
    if kv_prev is not None:
        aliases = {len(args): 1, len(args) + 1: 3}
        in_specs += [pl.BlockSpec(memory_space=pl.ANY)] * 2
        args += list(kv_prev)
    return pl.pallas_call(
        functools.partial(_inproj_kernel, lam_init=lam_init, has_prev=kv_prev is not None,
                          v_transposed=v_transposed),
        grid=(n // tm,),
        in_specs=in_specs,
        out_specs=out_specs,
        out_shape=out_shape,
        input_output_aliases=aliases,
        compiler_params=pltpu.CompilerParams(dimension_semantics=("arbitrary",), vmem_limit_bytes=VMEM_LIMIT),
        name="inproj",
    )(*args)


def _stack_q(q):
    lane = lax.broadcasted_iota(jnp.int32, q.shape, 1)
    zero = jnp.zeros_like(q)
    return jnp.concatenate([jnp.where(lane < A_HEAD_DIM, q, zero), jnp.where(lane >= A_HEAD_DIM, q, zero)], axis=0)


def _qk(qs, k):
    return lax.dot_general(qs, k, (((1,), (1,)), ((), ())), preferred_element_type=F32)


def _attn_finish(acc, l, lam_ref, sg_ref, tq, lam_init):
    o = acc[:tq] / l[:tq] - lam_ref[0:1, :] * (acc[tq:] / l[tq:])
    return _rms(o, sg_ref[...]) * (1.0 - lam_init)


def _attn_prompt_kernel(q_ref, k_ref, vt_ref, lam_ref, sg_ref, o_ref, qs_ref, s_ref, m_ref, acc_ref, *, tq, lam_init):
    qi = pl.program_id(1)
    for hd in range(A_HEADS):
        qs_ref[hd] = _stack_q(q_ref[:, hd * A_VDIM:(hd + 1) * A_VDIM])
    m_ref[...] = jnp.full(m_ref.shape, NEG_INF, F32)
    acc_ref[...] = jnp.zeros(acc_ref.shape, F32)

    def scores_into(j, hd):
        start = pl.multiple_of(j * tq, tq)
        s_ref[hd] = _qk(k_ref[pl.ds(start, tq), hd * A_VDIM:(hd + 1) * A_VDIM], qs_ref[hd])

    def softmax_pv(j, hd, visible):
        start = pl.multiple_of(j * tq, tq)
        st = s_ref[hd]
        if visible is not None:
            st = jnp.where(visible, st, NEG_INF)
        m_old = m_ref[hd]
        m_new = jnp.maximum(m_old, jnp.max(st, axis=0, keepdims=True))
        alpha = jnp.exp2(m_old - m_new)
        pt = jnp.exp2(st - m_new)
        m_ref[hd] = m_new
        acc_ref[hd] = alpha * acc_ref[hd] + jnp.dot(vt_ref[hd, :, pl.ds(start, tq)], pt.astype(BF16),
                                                    preferred_element_type=F32)

    for hd in range(A_HEADS):
        scores_into(0, hd)

    def body(j, carry):
        for hd in range(A_HEADS):
            softmax_pv(j, hd, None)
            scores_into(j + 1, hd)
        return carry

    lax.fori_loop(0, qi, body, 0)
    kc = lax.broadcasted_iota(jnp.int32, (tq, 2 * tq), 0) // CHUNK
    qc = (lax.broadcasted_iota(jnp.int32, (tq, 2 * tq), 1) % tq) // CHUNK
    visible = kc <= qc
    for hd in range(A_HEADS):
        softmax_pv(qi, hd, visible)
    lam = lam_ref[0:1, 0:1]
    for hd in range(A_HEADS):
        inv = 1.0 / acc_ref[hd, A_VDIM:A_VDIM + 1, :]
        acc = acc_ref[hd, 0:A_VDIM, :]
        ot = acc[:, :tq] * inv[:, :tq] - (lam * inv[:, tq:]) * acc[:, tq:]
        ot = ot * lax.rsqrt(jnp.mean(ot * ot, axis=0, keepdims=True) + EPS) * (sg_ref[...] * (1.0 - lam_init))
        o_ref[:, hd * A_VDIM:(hd + 1) * A_VDIM] = ot.T.astype(o_ref.dtype)


def _attn_prompt(q, k, vt, lam, sg, lam_init, *, bsz, seq, tq):
    q, k = (t.reshape(bsz, seq, A_WIDTH) for t in (q, k))
    qspec = pl.BlockSpec((None, tq, A_WIDTH), lambda b, i: (b, i, 0))
    kspec = pl.BlockSpec((None, seq, A_WIDTH), lambda b, i: (b, 0, 0))
    vspec = pl.BlockSpec((None, A_HEADS, VT_ROWS, seq), lambda b, i: (b, 0, 0, 0))
    out = pl.pallas_call(
        functools.partial(_attn_prompt_kernel, tq=tq, lam_init=lam_init),
        grid=(bsz, seq // tq),
        in_specs=[qspec, kspec, vspec,
                  pl.BlockSpec((8, A_VDIM), lambda b, i: (0, 0)),
                  pl.BlockSpec((A_VDIM, 1), lambda b, i: (0, 0))],
        out_specs=qspec,
        out_shape=jax.ShapeDtypeStruct((bsz, seq, A_WIDTH), BF16),
        scratch_shapes=[pltpu.VMEM((A_HEADS, 2 * tq, A_VDIM), BF16),
                        pltpu.VMEM((A_HEADS, tq, 2 * tq), F32),
                        pltpu.VMEM((A_HEADS, 1, 2 * tq), F32),
                        pltpu.VMEM((A_HEADS, VT_ROWS, 2 * tq), F32)],
        compiler_params=pltpu.CompilerParams(dimension_semantics=("arbitrary",) * 2, vmem_limit_bytes=VMEM_LIMIT),
        name="attn_prompt",
    )(q, k, vt, lam, sg.reshape(A_VDIM, 1))
    return out.reshape(bsz * seq, A_WIDTH)


def _attn_sample_kernel(q_ref, kc_ref, vc_ref, kn_ref, vn_ref, lam_ref, sg_ref, o_ref, *, tq, past, lam_init):
    for hd in range(A_HEADS):
        hs = slice(hd * A_VDIM, (hd + 1) * A_VDIM)
        qs = _stack_q(q_ref[:, hs])
        sc = _qk(qs, kc_ref[pl.ds(hd, past, stride=A_HEADS), :].astype(BF16))
        sn = _qk(qs, kn_ref[:, hs])
        m = jnp.maximum(jnp.max(sc, axis=-1, keepdims=True), jnp.max(sn, axis=-1, keepdims=True))
        pc = jnp.exp2(sc - m)
        pn = jnp.exp2(sn - m)
        l = jnp.sum(pc, axis=-1, keepdims=True) + jnp.sum(pn, axis=-1, keepdims=True)
        vc = vc_ref[pl.ds(hd, past, stride=A_HEADS), :].astype(BF16)
        acc = (jnp.dot(pc.astype(BF16), vc, preferred_element_type=F32)
               + jnp.dot(pn.astype(BF16), vn_ref[:, hs], preferred_element_type=F32))
        o_ref[:, hs] = _attn_finish(acc, l, lam_ref, sg_ref, tq, lam_init).astype(o_ref.dtype)


def _attn_sample(q, k, v, cache_k, cache_v, layer, lam, sg, lam_init, *, bsz, seq):
    past = cache_k.shape[2]
    q, k, v = (t.reshape(bsz, seq, A_WIDTH) for t in (q, k, v))
    ck = cache_k.reshape(DEPTH, bsz, past * A_HEADS, A_VDIM)
    cv = cache_v.reshape(DEPTH, bsz, past * A_HEADS, A_VDIM)
    nspec = pl.BlockSpec((None, seq, A_WIDTH), lambda b: (b, 0, 0))
    cspec = pl.BlockSpec((None, None, past * A_HEADS, A_VDIM), lambda b: (layer, b, 0, 0))
    out = pl.pallas_call(
        functools.partial(_attn_sample_kernel, tq=seq, past=past, lam_init=lam_init),
        grid=(bsz,),
        in_specs=[nspec, cspec, cspec, nspec, nspec,
                  pl.BlockSpec((8, A_VDIM), lambda b: (0, 0)),
                  pl.BlockSpec((1, A_VDIM), lambda b: (0, 0))],
        out_specs=nspec,
        out_shape=jax.ShapeDtypeStruct((bsz, seq, A_WIDTH), BF16),
        compiler_params=pltpu.CompilerParams(dimension_semantics=("arbitrary",), vmem_limit_bytes=VMEM_LIMIT),
        name="attn_sample",
    )(q, ck, cv, k, v, lam, sg.reshape(1, A_VDIM))
    return out.reshape(bsz * seq, A_WIDTH)


def _outproj_kernel(x_ref, ya_ref, u_ref, uh_ref, cv_ref, cvh_ref, bg_ref, wp_ref, ps_ref, cw_ref, wo_ref,
                    *rest, tm, tps, pos_off, zero_first, n_ffn):
    ffn_refs, (o_ref, uext_ref, cext_ref) = rest[:n_ffn], rest[n_ffn:]
    it = pl.program_id(0) % tps
    y = x_ref[...] + jnp.dot(ya_ref[...], wo_ref[0:A_WIDTH, :], preferred_element_type=F32)
    uh, cvh = uh_ref[...], cvh_ref[...]
    if zero_first:
        keep = (it > 0).astype(F32)
        uh, cvh = uh * keep, cvh * keep
    cur = u_ref[...]
    uext_ref[0:HALO, :] = uh
    uext_ref[HALO:, :] = cur
    cv = cv_ref[...]
    cext_ref[0:HALO, :] = cvh
    cext_ref[HALO:, :] = cv

    run = cur
    sums = {}
    for j in range(1, max(POOL_WINDOWS)):
        run = run + uext_ref[pl.ds(HALO - j, tm), :]
        if j + 1 in POOL_WINDOWS:
            sums[j + 1] = run
    pos = pos_off + it * tm + lax.broadcasted_iota(jnp.int32, (tm, 1), 0)
    lane = lax.broadcasted_iota(jnp.int32, (tm, B_WIDTH), 1)
    d = None
    for gi, w in reversed(list(enumerate(POOL_WINDOWS))):
        mean = sums[w] / jnp.minimum(pos + 1, w).astype(F32)
        d = mean if d is None else jnp.where(lane < (gi + 1) * POOL_GROUP, mean, d)
    d = d - cur
    y_pool = jnp.dot(d.astype(BF16), wp_ref[...], preferred_element_type=F32) * ps_ref[...]

    cw = cw_ref[...]
    conv = cw[0:1, :] * cext_ref[pl.ds(HALO - 2, tm), :] + cw[1:2, :] * cext_ref[pl.ds(HALO - 1, tm), :]
    conv = conv + cw[2:3, :] * cv
    y_conv = bg_ref[...] * conv

    mixed = jnp.concatenate([y_pool.astype(BF16), y_conv.astype(BF16)], axis=-1)
    y = y + jnp.dot(mixed, wo_ref[A_WIDTH:, :], preferred_element_type=F32)
    if ffn_refs:
        y = _ffn_body(y, *ffn_refs[:4], ffn_refs[4] if n_ffn == 5 else None, single_down=True)
    o_ref[...] = y


def _outproj(x, ya, u, cv, bg, u_halo, cv_halo, halo_map, wp, ps, cw, wo, layer, ffn=None, *, tm, seq, pos_off,
             zero_first):
    n, d = x.shape
    tps = seq // tm
    row = lambda w: pl.BlockSpec((tm, w), lambda i: (i, 0))
    halo = pl.BlockSpec((HALO, B_WIDTH), halo_map)
    in_specs = [row(d), row(A_WIDTH), row(B_WIDTH), halo, row(C_WIDTH), halo, row(C_WIDTH),
                _const_spec(wp.shape), _const_spec((1, B_WIDTH)), _const_spec(cw.shape), _layer_spec(wo, layer)]
    args = [x, ya, u, u_halo, cv, cv_halo, bg, wp, ps.reshape(1, B_WIDTH), cw, wo]
    if ffn is not None:
        g, wg, wu, wd, final_g = ffn
        in_specs += [_const_spec((1, d)), _layer_spec(wg, layer), _layer_spec(wu, layer), _layer_spec(wd, layer)]
        args += [g.reshape(1, d), wg, wu, wd]
        if final_g is not None:
            in_specs.append(_const_spec((1, d)))
            args.append(final_g.reshape(1, d))
    return pl.pallas_call(
        functools.partial(_outproj_kernel, tm=tm, tps=tps, pos_off=pos_off, zero_first=zero_first,
                          n_ffn=len(args) - 11),
        grid=(n // tm,),
        in_specs=in_specs,
        out_specs=row(d),
        out_shape=jax.ShapeDtypeStruct((n, d), F32),
        scratch_shapes=[pltpu.VMEM((HALO + tm, B_WIDTH), F32), pltpu.VMEM((HALO + tm, C_WIDTH), F32)],
        compiler_params=pltpu.CompilerParams(dimension_semantics=("arbitrary",), vmem_limit_bytes=VMEM_LIMIT),
        name="outproj_ffn" if ffn is not None else "outproj",
    )(*args)


def _block_diag(w):
    g, c, _ = w.shape
    eye = jnp.eye(g, dtype=w.dtype)
    return (eye[:, None, :, None] * w[:, :, None, :]).reshape(g * c, g * c)


def _front_pad(state, rows):
    b, r, c = state.shape
    return jnp.pad(state, ((0, 0), (rows - r, 0), (0, 0))).reshape(b * rows, c)


def _trunk(x, pos0, cache, p, *, tm_ffn, tm_mix, tq):
    bsz, seq, d = x.shape
    n = bsz * seq
    x = x.reshape(n, d)
    tables = _rope_tables(pos0 + np.arange(seq))
    kv_leaves, new_pool, new_conv = None, [], []
    for l in range(DEPTH):
        lam_init = 0.8 - 0.6 * math.exp(-0.3 * l)
        x = _ffn(x, p['norm_ffn1'][l], p['ffn1_gate'], p['ffn1_up'], p['ffn1_down'], l, tm=tm_ffn)
        lams = [p[name][l] for name in ('lambda_q1', 'lambda_k1', 'lambda_q2', 'lambda_k2')]
        q, k_all, kb, v_all, vb, u, cv, bg, lam = _inproj(x, p['norm_mix'][l], p['w_in'], tables, lams, lam_init,
                                                          l, kv_leaves, tm=tm_mix, seq=seq,
                                                          v_transposed=cache is None)
        kv_leaves = (k_all, v_all)
        if cache is None:
            ya = _attn_prompt(q, kb, vb, lam, p['subln'][l], lam_init, bsz=bsz, seq=seq, tq=tq)
            u_halo, cv_halo = u, cv
            per = tm_mix // HALO
            halo_map = lambda i, per=per: (jnp.maximum(i * per - 1, 0), 0)
            pool_ext, conv_ext = u.reshape(bsz, seq, B_WIDTH), cv.reshape(bsz, seq, C_WIDTH)
        else:
            cache_k, cache_v, state_pool, state_conv = cache
            ya = _attn_sample(q, kb, vb, cache_k, cache_v, l, lam, p['subln'][l], lam_init, bsz=bsz, seq=seq)
            u_halo = _front_pad(state_pool[l], HALO)
            cv_halo = _front_pad(state_conv[l], HALO)
            halo_map = lambda i: (i, 0)
            pool_ext = jnp.concatenate([state_pool[l], u.reshape(bsz, seq, B_WIDTH)], axis=1)
            conv_ext = jnp.concatenate([state_conv[l], cv.reshape(bsz, seq, C_WIDTH)], axis=1)
        final_g = p['final_norm'] if l == DEPTH - 1 else None
        ffn2 = (p['norm_ffn2'][l], p['ffn2_gate'], p['ffn2_up'], p['ffn2_down'], final_g)
        fuse = tm_mix == tm_ffn
        x = _outproj(x, ya, u, cv, bg, u_halo, cv_halo, halo_map,
                     p['pool_w_bd'][l], p['pool_scale'][l], p['conv_w'][l], p['w_out'], l, ffn2 if fuse else None,
                     tm=tm_mix, seq=seq, pos_off=pos0, zero_first=cache is None)
        if not fuse:
            x = _ffn(x, *ffn2[:4], l, final_g, tm=tm_ffn)
        new_pool.append(pool_ext[:, -POOL_HIST:])
        new_conv.append(conv_ext[:, -(CONV_W - 1):])
    new_k, new_v = (t.reshape(DEPTH, bsz, seq, A_HEADS, A_VDIM) for t in kv_leaves)
    return (x.reshape(bsz, seq, d), new_k, new_v, jnp.stack(new_pool), jnp.stack(new_conv))


def kernel(x_prompt, x_sample, cache_k, cache_v, state_pool, state_conv, norm_ffn1, ffn1_gate, ffn1_up, ffn1_down,
           norm_mix, w_in, lambda_q1, lambda_k1, lambda_q2, lambda_k2, subln, pool_w, pool_scale, conv_w, w_out,
           norm_ffn2, ffn2_gate, ffn2_up, ffn2_down, final_norm):
    p = {
        'norm_ffn1': norm_ffn1, 'norm_mix': norm_mix, 'norm_ffn2': norm_ffn2, 'final_norm': final_norm,
        'lambda_q1': lambda_q1, 'lambda_k1': lambda_k1, 'lambda_q2': lambda_q2, 'lambda_k2': lambda_k2,
        'subln': subln, 'pool_scale': pool_scale, 'conv_w': conv_w,
        'ffn1_gate': ffn1_gate, 'ffn1_up': ffn1_up, 'ffn1_down': ffn1_down,
        'ffn2_gate': ffn2_gate, 'ffn2_up': ffn2_up, 'ffn2_down': ffn2_down,
        'w_in': w_in, 'w_out': w_out.astype(BF16),
        'pool_w_bd': jnp.stack([_block_diag(pool_w[l]) for l in range(DEPTH)]).astype(BF16),
    }
    y_p, k_p, v_p, pool_p, conv_p = _trunk(x_prompt, 0, None, p, **PROMPT_TILES)
    y_s, k_s, v_s, pool_s, conv_s = _trunk(x_sample, PAST_LEN, (cache_k, cache_v, state_pool, state_conv), p,
                                           **SAMPLE_TILES)
    return (y_p, y_s, k_p, v_p, pool_p, conv_p, k_s, v_s, pool_s, conv_s)
```

```python
import functools
import math

import numpy as np
import jax
import jax.numpy as jnp
from jax import lax
from jax.experimental import pallas as pl
from jax.experimental.pallas import tpu as pltpu

F32 = jnp.float32
BF16 = jnp.bfloat16

D_MODEL = 1024
DEPTH = 2
PAST_LEN = 4096
CHUNK = 64
A_WIDTH = 512
A_HEAD_DIM = 64
A_HEADS = 4
A_VDIM = 128
ATTN_SCALE = A_HEAD_DIM ** -0.5
B_WIDTH = 256
POOL_WINDOWS = (2, 4, 8, 16)
POOL_GROUP = 64
POOL_HIST = 15
C_WIDTH = 256
CONV_W = 3
D_FF = 2816
ROPE_THETA = 10000.0
EPS = 1e-6
NEG_INF = -1e30
D_IN = 3 * A_WIDTH + B_WIDTH + 3 * C_WIDTH

LOG2E = math.log2(math.e)
BF16_ROWS = 16
VT_ROWS = A_VDIM + BF16_ROWS
HALO = 16
FF_CHUNK = 256
VMEM_LIMIT = 56 * 1024 * 1024
PROMPT_TILES = dict(tm_ffn=512, tm_mix=512, tq=256)
SAMPLE_TILES = dict(tm_ffn=256, tm_mix=32, tq=None)


def _rms(x, g):
    return x * lax.rsqrt(jnp.mean(x * x, axis=-1, keepdims=True) + EPS) * g


def _const_spec(shape):
    return pl.BlockSpec(shape, lambda *_: (0,) * len(shape), pipeline_mode=pl.Buffered(1))


def _layer_spec(w, layer):
    tail = w.shape[1:]
    return pl.BlockSpec((None,) + tail, lambda *_: (layer,) + (0,) * len(tail), pipeline_mode=pl.Buffered(1))


def _ffn_kernel(x_ref, g_ref, wg_ref, wu_ref, wd_ref, *rest, final_norm):
    fg_ref = rest[0] if final_norm else None
    rest[-1][...] = _ffn_body(x_ref[...], g_ref, wg_ref, wu_ref, wd_ref, fg_ref, single_down=True)


def _ffn_body(x, g_ref, wg_ref, wu_ref, wd_ref, fg_ref, single_down=False):
    h = _rms(x, g_ref[...]).astype(BF16)
    acc = jnp.zeros(x.shape, F32)
    acts = []
    for c in range(D_FF // FF_CHUNK):
        sl = slice(c * FF_CHUNK, (c + 1) * FF_CHUNK)
        gate = jnp.dot(h, wg_ref[:, sl].astype(BF16), preferred_element_type=F32)
        up = jnp.dot(h, wu_ref[:, sl].astype(BF16), preferred_element_type=F32)
        act = (gate * (1.0 / (1.0 + jnp.exp(-gate))) * up).astype(BF16)
        if single_down:
            acts.append(act)
        else:
            acc = acc + jnp.dot(act, wd_ref[sl, :].astype(BF16), preferred_element_type=F32)
    if single_down:
        acc = jnp.dot(jnp.concatenate(acts, axis=-1), wd_ref[...].astype(BF16), preferred_element_type=F32)
    y = x + 0.5 * acc
    return y if fg_ref is None else _rms(y, fg_ref[...])


def _ffn_stream_kernel(x_ref, g_ref, wg_ref, wu_ref, wd_ref, *rest, final_norm):
    fg_ref = rest[0] if final_norm else None
    o_ref, h_ref, acc_ref = rest[-3:]
    c = pl.program_id(0)

    @pl.when(c == 0)
    def _():
        h_ref[...] = _rms(x_ref[...], g_ref[...]).astype(BF16)
        acc_ref[...] = jnp.zeros(acc_ref.shape, F32)

    h = h_ref[...]
    gate = jnp.dot(h, wg_ref[...].astype(BF16), preferred_element_type=F32)
    up = jnp.dot(h, wu_ref[...].astype(BF16), preferred_element_type=F32)
    act = (gate * (1.0 / (1.0 + jnp.exp(-gate))) * up).astype(BF16)
    acc_ref[...] += jnp.dot(act, wd_ref[...].astype(BF16), preferred_element_type=F32)

    @pl.when(c == pl.num_programs(0) - 1)
    def _():
        y = x_ref[...] + 0.5 * acc_ref[...]
        o_ref[...] = y if fg_ref is None else _rms(y, fg_ref[...])


def _ffn_stream(x, g, wg, wu, wd, layer, final_g=None):
    n, d = x.shape
    whole = pl.BlockSpec((n, d), lambda c: (0, 0))
    one = pl.BlockSpec((1, d), lambda c: (0, 0))
    in_specs = [whole, one,
                pl.BlockSpec((None, d, FF_CHUNK), lambda c: (layer, 0, c)),
                pl.BlockSpec((None, d, FF_CHUNK), lambda c: (layer, 0, c)),
                pl.BlockSpec((None, FF_CHUNK, d), lambda c: (layer, c, 0))]
    args = [x, g.reshape(1, d), wg, wu, wd]
    if final_g is not None:
        in_specs.append(one)
        args.append(final_g.reshape(1, d))
    return pl.pallas_call(
        functools.partial(_ffn_stream_kernel, final_norm=final_g is not None),
        grid=(D_FF // FF_CHUNK,),
        in_specs=in_specs,
        out_specs=whole,
        out_shape=jax.ShapeDtypeStruct((n, d), F32),
        scratch_shapes=[pltpu.VMEM((n, d), BF16), pltpu.VMEM((n, d), F32)],
        compiler_params=pltpu.CompilerParams(dimension_semantics=("arbitrary",), vmem_limit_bytes=VMEM_LIMIT),
        name="ffn_stream",
    )(*args)


def _ffn(x, g, wg, wu, wd, layer, final_g=None, *, tm):
    n, d = x.shape
    assert n % tm == 0
    if n == tm:
        return _ffn_stream(x, g, wg, wu, wd, layer, final_g)
    row = pl.BlockSpec((tm, d), lambda i: (i, 0))
    in_specs = [row, _const_spec((1, d)), _layer_spec(wg, layer), _layer_spec(wu, layer), _layer_spec(wd, layer)]
    args = [x, g.reshape(1, d), wg, wu, wd]
    if final_g is not None:
        in_specs.append(_const_spec((1, d)))
        args.append(final_g.reshape(1, d))
    return pl.pallas_call(
        functools.partial(_ffn_kernel, final_norm=final_g is not None),
        grid=(n // tm,),
        in_specs=in_specs,
        out_specs=row,
        out_shape=jax.ShapeDtypeStruct((n, d), F32),
        compiler_params=pltpu.CompilerParams(dimension_semantics=("arbitrary",), vmem_limit_bytes=VMEM_LIMIT),
        name="ffn",
    )(*args)


def _inproj_kernel(x_ref, g_ref, w_ref, cos_ref, sa_ref, sb_ref, lq1_ref, lk1_ref, lq2_ref, lk2_ref, *rest,
                   lam_init, has_prev, v_transposed):
    q_ref, k_ref, kb_ref, v_ref, vb_ref, u_ref, cv_ref, bg_ref, lam_ref = rest[2:] if has_prev else rest
    tm = x_ref.shape[0]

    @pl.when(pl.program_id(0) == 0)
    def _():
        a = jnp.sum(lq1_ref[...] * lk1_ref[...], axis=-1, keepdims=True)
        b = jnp.sum(lq2_ref[...] * lk2_ref[...], axis=-1, keepdims=True)
        lam_ref[...] = jnp.broadcast_to(jnp.exp(a) - jnp.exp(b) + lam_init, lam_ref.shape)

    h = _rms(x_ref[...], g_ref[...]).astype(BF16)
    z = jnp.dot(h, w_ref[...].astype(BF16), preferred_element_type=F32)
    z_qk, z = z[:, 0:2 * A_WIDTH], z[:, 2 * A_WIDTH:]
    cos, sa, sb = cos_ref[...], sa_ref[...], sb_ref[...]
    half = A_HEAD_DIM // 2

    def rope(t):
        return t * cos + pltpu.roll(t, A_VDIM - half, 1) * sa + pltpu.roll(t, half, 1) * sb

    for hd in range(A_HEADS):
        sl = slice(hd * A_VDIM, (hd + 1) * A_VDIM)
        qh = rope(z_qk[:, hd * A_VDIM:(hd + 1) * A_VDIM])
        q_ref[:, sl] = (qh * (ATTN_SCALE * LOG2E)).astype(BF16)
        kh = rope(z_qk[:, A_WIDTH + hd * A_VDIM:A_WIDTH + (hd + 1) * A_VDIM])
        k_ref[pl.ds(hd, tm, stride=A_HEADS), :] = kh
        kb_ref[:, sl] = kh.astype(BF16)
    v = z[:, 0:A_WIDTH]
    for hd in range(A_HEADS):
        v_ref[pl.ds(hd, tm, stride=A_HEADS), :] = v[:, hd * A_VDIM:(hd + 1) * A_VDIM]
    if v_transposed:
        vt = v.T.astype(BF16)
        for hd in range(A_HEADS):
            vb_ref[hd, 0:A_VDIM, :] = vt[hd * A_VDIM:(hd + 1) * A_VDIM, :]
            vb_ref[hd, A_VDIM:VT_ROWS, :] = jnp.ones((BF16_ROWS, tm), BF16)
    else:
        vb_ref[...] = v.astype(BF16)
    o = A_WIDTH
    u_ref[...] = z[:, o:o + B_WIDTH]
    hc = z[:, o + B_WIDTH:o + B_WIDTH + C_WIDTH]
    bg_ref[...] = z[:, o + B_WIDTH + C_WIDTH:o + B_WIDTH + 2 * C_WIDTH]
    cg = z[:, o + B_WIDTH + 2 * C_WIDTH:o + B_WIDTH + 3 * C_WIDTH]
    cv_ref[...] = cg * hc


def _rope_tables(pos):
    half = A_HEAD_DIM // 2
    inv = np.power(ROPE_THETA, -np.arange(half, dtype=np.float64) / half)
    ang = pos.astype(np.float64)[:, None] * inv[None, :]
    cos, sin, zero = np.cos(ang), np.sin(ang), np.zeros_like(ang)
    cos_t = np.concatenate([cos, cos, cos, cos], axis=1)
    sa = np.concatenate([-sin, zero, -sin, zero], axis=1)
    sb = np.concatenate([zero, sin, zero, sin], axis=1)
    return [jnp.asarray(t, F32) for t in (cos_t, sa, sb)]


def _inproj(x, g, w_in, tables, lams, lam_init, layer, kv_prev, *, tm, seq, v_transposed):
    n, d = x.shape
    tps = seq // tm
    if v_transposed:
        vb_shape = jax.ShapeDtypeStruct((n // seq, A_HEADS, VT_ROWS, seq), BF16)
        vb_spec = pl.BlockSpec((None, A_HEADS, VT_ROWS, tm), lambda i: (i // tps, 0, 0, i % tps))
    else:
        vb_shape = jax.ShapeDtypeStruct((n, A_WIDTH), BF16)
        vb_spec = pl.BlockSpec((tm, A_WIDTH), lambda i: (i, 0))
    row = lambda w: pl.BlockSpec((tm, w), lambda i: (i, 0))
    tab = pl.BlockSpec((tm, A_VDIM), lambda i: (i % tps, 0))
    lam_spec = _const_spec((1, A_HEAD_DIM))
    leaf = jax.ShapeDtypeStruct((DEPTH, n * A_HEADS, A_VDIM), F32)
    leaf_spec = pl.BlockSpec((None, tm * A_HEADS, A_VDIM), lambda i: (layer, i, 0))
    out_shape = [
        jax.ShapeDtypeStruct((n, A_WIDTH), BF16),
        leaf,
        jax.ShapeDtypeStruct((n, A_WIDTH), BF16),
        leaf,
        vb_shape,
        jax.ShapeDtypeStruct((n, B_WIDTH), F32),
        jax.ShapeDtypeStruct((n, C_WIDTH), F32),
        jax.ShapeDtypeStruct((n, C_WIDTH), F32),
        jax.ShapeDtypeStruct((8, A_VDIM), F32),
    ]
    out_specs = [row(A_WIDTH), leaf_spec, row(A_WIDTH), leaf_spec, vb_spec, row(B_WIDTH), row(C_WIDTH),
                 row(C_WIDTH), pl.BlockSpec((8, A_VDIM), lambda i: (0, 0))]
    in_specs = [row(d), _const_spec((1, d)), _layer_spec(w_in, layer), tab, tab, tab] + [lam_spec] * 4
    args = [x, g.reshape(1, d), w_in, *tables, *[v.reshape(1, A_HEAD_DIM) for v in lams]]
    aliases = {}
    if kv_prev is not None:
        aliases = {len(args): 1, len(args) + 1: 3}
        in_specs += [pl.BlockSpec(memory_space=pl.ANY)] * 2
        args += list(kv_prev)
    return pl.pallas_call(
        functools.partial(_inproj_kernel, lam_init=lam_init, has_prev=kv_prev is not None,
                          v_transposed=v_transposed),
        grid=(n // tm,),
        in_specs=in_specs,
        out_specs=out_specs,
        out_shape=out_shape,
        input_output_aliases=aliases,
        compiler_params=pltpu.CompilerParams(dimension_semantics=("arbitrary",), vmem_limit_bytes=VMEM_LIMIT),
        name="inproj",
    )(*args)


def _stack_q(q):
    lane = lax.broadcasted_iota(jnp.int32, q.shape, 1)
    zero = jnp.zeros_like(q)
    return jnp.concatenate([jnp.where(lane < A_HEAD_DIM, q, zero), jnp.where(lane >= A_HEAD_DIM, q, zero)], axis=0)


def _qk(qs, k):
    return lax.dot_general(qs, k, (((1,), (1,)), ((), ())), preferred_element_type=F32)


def _attn_finish(acc, l, lam_ref, sg_ref, tq, lam_init):
    o = acc[:tq] / l[:tq] - lam_ref[0:1, :] * (acc[tq:] / l[tq:])
    return _rms(o, sg_ref[...]) * (1.0 - lam_init)


def _attn_prompt_kernel(q_ref, k_ref, vt_ref, lam_ref, sg_ref, o_ref, qs_ref, s_ref, m_ref, acc_ref, *, tq, lam_init):
    qi = pl.program_id(1)
    for hd in range(A_HEADS):
        qs_ref[hd] = _stack_q(q_ref[:, hd * A_VDIM:(hd + 1) * A_VDIM])
    m_ref[...] = jnp.full(m_ref.shape, NEG_INF, F32)
    acc_ref[...] = jnp.zeros(acc_ref.shape, F32)

    def scores_into(j, hd):
        start = pl.multiple_of(j * tq, tq)
        s_ref[hd] = _qk(k_ref[pl.ds(start, tq), hd * A_VDIM:(hd + 1) * A_VDIM], qs_ref[hd])

    def softmax_pv(j, hd, visible):
        start = pl.multiple_of(j * tq, tq)
        st = s_ref[hd]
        if visible is not None:
            st = jnp.where(visible, st, NEG_INF)
        m_old = m_ref[hd]
        m_new = jnp.maximum(m_old, jnp.max(st, axis=0, keepdims=True))
        alpha = jnp.exp2(m_old - m_new)
        pt = jnp.exp2(st - m_new)
        m_ref[hd] = m_new
        acc_ref[hd] = alpha * acc_ref[hd] + jnp.dot(vt_ref[hd, :, pl.ds(start, tq)], pt.astype(BF16),
                                                    preferred_element_type=F32)

    for hd in range(A_HEADS):
        scores_into(0, hd)

    def body(j, carry):
        for hd in range(A_HEADS):
            softmax_pv(j, hd, None)
            scores_into(j + 1, hd)
        return carry

    lax.fori_loop(0, qi, body, 0)
    kc = lax.broadcasted_iota(jnp.int32, (tq, 2 * tq), 0) // CHUNK
    qc = (lax.broadcasted_iota(jnp.int32, (tq, 2 * tq), 1) % tq) // CHUNK
    visible = kc <= qc
    for hd in range(A_HEADS):
        softmax_pv(qi, hd, visible)
    lam = lam_ref[0:1, 0:1]
    for hd in range(A_HEADS):
        inv = 1.0 / acc_ref[hd, A_VDIM:A_VDIM + 1, :]
        acc = acc_ref[hd, 0:A_VDIM, :]
        ot = acc[:, :tq] * inv[:, :tq] - (lam * inv[:, tq:]) * acc[:, tq:]
        ot = ot * lax.rsqrt(jnp.mean(ot * ot, axis=0, keepdims=True) + EPS) * (sg_ref[...] * (1.0 - lam_init))
        o_ref[:, hd * A_VDIM:(hd + 1) * A_VDIM] = ot.T.astype(o_ref.dtype)


def _attn_prompt(q, k, vt, lam, sg, lam_init, *, bsz, seq, tq):
    q, k = (t.reshape(bsz, seq, A_WIDTH) for t in (q, k))
    qspec = pl.BlockSpec((None, tq, A_WIDTH), lambda b, i: (b, i, 0))
    kspec = pl.BlockSpec((None, seq, A_WIDTH), lambda b, i: (b, 0, 0))
    vspec = pl.BlockSpec((None, A_HEADS, VT_ROWS, seq), lambda b, i: (b, 0, 0, 0))
    out = pl.pallas_call(
        functools.partial(_attn_prompt_kernel, tq=tq, lam_init=lam_init),
        grid=(bsz, seq // tq),
        in_specs=[qspec, kspec, vspec,
                  pl.BlockSpec((8, A_VDIM), lambda b, i: (0, 0)),
                  pl.BlockSpec((A_VDIM, 1), lambda b, i: (0, 0))],
        out_specs=qspec,
        out_shape=jax.ShapeDtypeStruct((bsz, seq, A_WIDTH), BF16),
        scratch_shapes=[pltpu.VMEM((A_HEADS, 2 * tq, A_VDIM), BF16),
                        pltpu.VMEM((A_HEADS, tq, 2 * tq), F32),
                        pltpu.VMEM((A_HEADS, 1, 2 * tq), F32),
                        pltpu.VMEM((A_HEADS, VT_ROWS, 2 * tq), F32)],
        compiler_params=pltpu.CompilerParams(dimension_semantics=("arbitrary",) * 2, vmem_limit_bytes=VMEM_LIMIT),
        name="attn_prompt",
    )(q, k, vt, lam, sg.reshape(A_VDIM, 1))
    return out.reshape(bsz * seq, A_WIDTH)


def _attn_sample_kernel(q_ref, kc_ref, vc_ref, kn_ref, vn_ref, lam_ref, sg_ref, o_ref, *, tq, past, lam_init):
    for hd in range(A_HEADS):
        hs = slice(hd * A_VDIM, (hd + 1) * A_VDIM)
        qs = _stack_q(q_ref[:, hs])
        sc = _qk(qs, kc_ref[pl.ds(hd, past, stride=A_HEADS), :].astype(BF16))
        sn = _qk(qs, kn_ref[:, hs])
        m = jnp.maximum(jnp.max(sc, axis=-1, keepdims=True), jnp.max(sn, axis=-1, keepdims=True))
        pc = jnp.exp2(sc - m)
        pn = jnp.exp2(sn - m)
        l = jnp.sum(pc, axis=-1, keepdims=True) + jnp.sum(pn, axis=-1, keepdims=True)
        vc = vc_ref[pl.ds(hd, past, stride=A_HEADS), :].astype(BF16)
        acc = (jnp.dot(pc.astype(BF16), vc, preferred_element_type=F32)
               + jnp.dot(pn.astype(BF16), vn_ref[:, hs], preferred_element_type=F32))
        o_ref[:, hs] = _attn_finish(acc, l, lam_ref, sg_ref, tq, lam_init).astype(o_ref.dtype)


def _attn_sample(q, k, v, cache_k, cache_v, layer, lam, sg, lam_init, *, bsz, seq):
    past = cache_k.shape[2]
    q, k, v = (t.reshape(bsz, seq, A_WIDTH) for t in (q, k, v))
    ck = cache_k.reshape(DEPTH, bsz, past * A_HEADS, A_VDIM)
    cv = cache_v.reshape(DEPTH, bsz, past * A_HEADS, A_VDIM)
    nspec = pl.BlockSpec((None, seq, A_WIDTH), lambda b: (b, 0, 0))
    cspec = pl.BlockSpec((None, None, past * A_HEADS, A_VDIM), lambda b: (layer, b, 0, 0))
    out = pl.pallas_call(
        functools.partial(_attn_sample_kernel, tq=seq, past=past, lam_init=lam_init),
        grid=(bsz,),
        in_specs=[nspec, cspec, cspec, nspec, nspec,
                  pl.BlockSpec((8, A_VDIM), lambda b: (0, 0)),
                  pl.BlockSpec((1, A_VDIM), lambda b: (0, 0))],
        out_specs=nspec,
        out_shape=jax.ShapeDtypeStruct((bsz, seq, A_WIDTH), BF16),
        compiler_params=pltpu.CompilerParams(dimension_semantics=("arbitrary",), vmem_limit_bytes=VMEM_LIMIT),
        name="attn_sample",
    )(q, ck, cv, k, v, lam, sg.reshape(1, A_VDIM))
    return out.reshape(bsz * seq, A_WIDTH)


def _outproj_kernel(x_ref, ya_ref, u_ref, uh_ref, cv_ref, cvh_ref, bg_ref, wp_ref, ps_ref, cw_ref, wo_ref,
                    *rest, tm, tps, pos_off, zero_first, n_ffn):
    ffn_refs, (o_ref, uext_ref, cext_ref) = rest[:n_ffn], rest[n_ffn:]
    it = pl.program_id(0) % tps
    y = x_ref[...] + jnp.dot(ya_ref[...], wo_ref[0:A_WIDTH, :], preferred_element_type=F32)
    uh, cvh = uh_ref[...], cvh_ref[...]
    if zero_first:
        keep = (it > 0).astype(F32)
        uh, cvh = uh * keep, cvh * keep
    cur = u_ref[...]
    uext_ref[0:HALO, :] = uh
    uext_ref[HALO:, :] = cur
    cv = cv_ref[...]
    cext_ref[0:HALO, :] = cvh
    cext_ref[HALO:, :] = cv

    run = cur
    sums = {}
    for j in range(1, max(POOL_WINDOWS)):
        run = run + uext_ref[pl.ds(HALO - j, tm), :]
        if j + 1 in POOL_WINDOWS:
            sums[j + 1] = run
    pos = pos_off + it * tm + lax.broadcasted_iota(jnp.int32, (tm, 1), 0)
    lane = lax.broadcasted_iota(jnp.int32, (tm, B_WIDTH), 1)
    d = None
    for gi, w in reversed(list(enumerate(POOL_WINDOWS))):
        mean = sums[w] / jnp.minimum(pos + 1, w).astype(F32)
        d = mean if d is None else jnp.where(lane < (gi + 1) * POOL_GROUP, mean, d)
    d = d - cur
    y_pool = jnp.dot(d.astype(BF16), wp_ref[...], preferred_element_type=F32) * ps_ref[...]

    cw = cw_ref[...]
    conv = cw[0:1, :] * cext_ref[pl.ds(HALO - 2, tm), :] + cw[1:2, :] * cext_ref[pl.ds(HALO - 1, tm), :]
    conv = conv + cw[2:3, :] * cv
    y_conv = bg_ref[...] * conv

    mixed = jnp.concatenate([y_pool.astype(BF16), y_conv.astype(BF16)], axis=-1)
    y = y + jnp.dot(mixed, wo_ref[A_WIDTH:, :], preferred_element_type=F32)
    if ffn_refs:
        y = _ffn_body(y, *ffn_refs[:4], ffn_refs[4] if n_ffn == 5 else None, single_down=True)
    o_ref[...] = y


def _outproj(x, ya, u, cv, bg, u_halo, cv_halo, halo_map, wp, ps, cw, wo, layer, ffn=None, *, tm, seq, pos_off,
             zero_first):
    n, d = x.shape
    tps = seq // tm
    row = lambda w: pl.BlockSpec((tm, w), lambda i: (i, 0))
    halo = pl.BlockSpec((HALO, B_WIDTH), halo_map)
    in_specs = [row(d), row(A_WIDTH), row(B_WIDTH), halo, row(C_WIDTH), halo, row(C_WIDTH),
                _const_spec(wp.shape), _const_spec((1, B_WIDTH)), _const_spec(cw.shape), _layer_spec(wo, layer)]
    args = [x, ya, u, u_halo, cv, cv_halo, bg, wp, ps.reshape(1, B_WIDTH), cw, wo]
    if ffn is not None:
        g, wg, wu, wd, final_g = ffn
        in_specs += [_const_spec((1, d)), _layer_spec(wg, layer), _layer_spec(wu, layer), _layer_spec(wd, layer)]
        args += [g.reshape(1, d), wg, wu, wd]
        if final_g is not None:
            in_specs.append(_const_spec((1, d)))
            args.append(final_g.reshape(1, d))
    return pl.pallas_call(
        functools.partial(_outproj_kernel, tm=tm, tps=tps, pos_off=pos_off, zero_first=zero_first,
                          n_ffn=len(args) - 11),
        grid=(n // tm,),
        in_specs=in_specs,
        out_specs=row(d),
        out_shape=jax.ShapeDtypeStruct((n, d), F32),
        scratch_shapes=[pltpu.VMEM((HALO + tm, B_WIDTH), F32), pltpu.VMEM((HALO + tm, C_WIDTH), F32)],
        compiler_params=pltpu.CompilerParams(dimension_semantics=("arbitrary",), vmem_limit_bytes=VMEM_LIMIT),
        name="outproj_ffn" if ffn is not None else "outproj",
    )(*args)


def _block_diag(w):
    g, c, _ = w.shape
    eye = jnp.eye(g, dtype=w.dtype)
    return (eye[:, None, :, None] * w[:, :, None, :]).reshape(g * c, g * c)


def _front_pad(state, rows):
    b, r, c = state.shape
    return jnp.pad(state, ((0, 0), (rows - r, 0), (0, 0))).reshape(b * rows, c)


def _trunk(x, pos0, cache, p, *, tm_ffn, tm_mix, tq):
    bsz, seq, d = x.shape
    n = bsz * seq
    x = x.reshape(n, d)
    tables = _rope_tables(pos0 + np.arange(seq))
    kv_leaves, new_pool, new_conv = None, [], []
    for l in range(DEPTH):
        lam_init = 0.8 - 0.6 * math.exp(-0.3 * l)
        x = _ffn(x, p['norm_ffn1'][l], p['ffn1_gate'], p['ffn1_up'], p['ffn1_down'], l, tm=tm_ffn)
        lams = [p[name][l] for name in ('lambda_q1', 'lambda_k1', 'lambda_q2', 'lambda_k2')]
        q, k_all, kb, v_all, vb, u, cv, bg, lam = _inproj(x, p['norm_mix'][l], p['w_in'], tables, lams, lam_init,
                                                          l, kv_leaves, tm=tm_mix, seq=seq,
                                                          v_transposed=cache is None)
        kv_leaves = (k_all, v_all)
        if cache is None:
            ya = _attn_prompt(q, kb, vb, lam, p['subln'][l], lam_init, bsz=bsz, seq=seq, tq=tq)
            u_halo, cv_halo = u, cv
            per = tm_mix // HALO
            halo_map = lambda i, per=per: (jnp.maximum(i * per - 1, 0), 0)
            pool_ext, conv_ext = u.reshape(bsz, seq, B_WIDTH), cv.reshape(bsz, seq, C_WIDTH)
        else:
            cache_k, cache_v, state_pool, state_conv = cache
            ya = _attn_sample(q, kb, vb, cache_k, cache_v, l, lam, p['subln'][l], lam_init, bsz=bsz, seq=seq)
            u_halo = _front_pad(state_pool[l], HALO)
            cv_halo = _front_pad(state_conv[l], HALO)
            halo_map = lambda i: (i, 0)
            pool_ext = jnp.concatenate([state_pool[l], u.reshape(bsz, seq, B_WIDTH)], axis=1)
            conv_ext = jnp.concatenate([state_conv[l], cv.reshape(bsz, seq, C_WIDTH)], axis=1)
        final_g = p['final_norm'] if l == DEPTH - 1 else None
        ffn2 = (p['norm_ffn2'][l], p['ffn2_gate'], p['ffn2_up'], p['ffn2_down'], final_g)
        fuse = tm_mix == tm_ffn
        x = _outproj(x, ya, u, cv, bg, u_halo, cv_halo, halo_map,
                     p['pool_w_bd'][l], p['pool_scale'][l], p['conv_w'][l], p['w_out'], l, ffn2 if fuse else None,
                     tm=tm_mix, seq=seq, pos_off=pos0, zero_first=cache is None)
        if not fuse:
            x = _ffn(x, *ffn2[:4], l, final_g, tm=tm_ffn)
        new_pool.append(pool_ext[:, -POOL_HIST:])
        new_conv.append(conv_ext[:, -(CONV_W - 1):])
    new_k, new_v = (t.reshape(DEPTH, bsz, seq, A_HEADS, A_VDIM) for t in kv_leaves)
    return (x.reshape(bsz, seq, d), new_k, new_v, jnp.stack(new_pool), jnp.stack(new_conv))


def kernel(x_prompt, x_sample, cache_k, cache_v, state_pool, state_conv, norm_ffn1, ffn1_gate, ffn1_up, ffn1_down,
           norm_mix, w_in, lambda_q1, lambda_k1, lambda_q2, lambda_k2, subln, pool_w, pool_scale, conv_w, w_out,
           norm_ffn2, ffn2_gate, ffn2_up, ffn2_down, final_norm):
    p = {
        'norm_ffn1': norm_ffn1, 'norm_mix': norm_mix, 'norm_ffn2': norm_ffn2, 'final_norm': final_norm,
        'lambda_q1': lambda_q1, 'lambda_k1': lambda_k1, 'lambda_q2': lambda_q2, 'lambda_k2': lambda_k2,
        'subln': subln, 'pool_scale': pool_scale, 'conv_w': conv_w,
        'ffn1_gate': ffn1_gate, 'ffn1_up': ffn1_up, 'ffn1_down': ffn1_down,
        'ffn2_gate': ffn2_gate, 'ffn2_up': ffn2_up, 'ffn2_down': ffn2_down,
        'w_in': w_in, 'w_out': w_out.astype(BF16),
        'pool_w_bd': jnp.stack([_block_diag(pool_w[l]) for l in range(DEPTH)]).astype(BF16),
    }
    y_p, k_p, v_p, pool_p, conv_p = _trunk(x_prompt, 0, None, p, **PROMPT_TILES)
    y_s, k_s, v_s, pool_s, conv_s = _trunk(x_sample, PAST_LEN, (cache_k, cache_v, state_pool, state_conv), p,
                                           **SAMPLE_TILES)
    return (y_p, y_s, k_p, v_p, pool_p, conv_p, k_s, v_s, pool_s, conv_s)
```
